```python
import jax, jax.numpy as jnp
from jax import lax
import numpy as np

D_MODEL = 2048
BATCH = 1
SEQ = 8192
DEPTH = 2
DEC_BATCH = 128
DEC_SEQ = 1
PAST_LEN = 2048
PAGE_SIZE = 128

N_A_LAYERS = DEPTH // 2
N_B_LAYERS = DEPTH - N_A_LAYERS
RMS_EPS = 1e-6
ROPE_THETA = 10000.0

GDN_HEADS = 16
GDN_DK = D_MODEL // GDN_HEADS
GDN_DV = GDN_DK
GDN_QKV = 2 * GDN_HEADS * GDN_DK + GDN_HEADS * GDN_DV
GDN_CONV = 4
GDN_CHUNK = 64

NSA_HEADS = 16
NSA_HEAD_DIM = D_MODEL // NSA_HEADS
NSA_KV_GROUPS = 4
NSA_HPG = NSA_HEADS // NSA_KV_GROUPS
CMP_STRIDE = 16
CMP_BLOCK = 2 * CMP_STRIDE
CMP_HIDDEN = 2 * NSA_HEAD_DIM
SEL_BLOCK = 64
N_SELECT = 16
N_LOCAL = 2
WINDOW = 512
Q_BLOCK = 128
SEL_FORCE = 1e9

D_FF = 5632
FFN_CONV = 3

kernel_name = 'yoco_gdn_nsa_convffn_step'


def rmsnorm(x, w):
    xf = x.astype(jnp.float32)
    y = xf * lax.rsqrt(jnp.mean(xf * xf, axis=-1, keepdims=True) + RMS_EPS)
    return (y * w.astype(jnp.float32)).astype(x.dtype)


def l2norm(x):
    xf = x.astype(jnp.float32)
    return xf * lax.rsqrt(jnp.sum(xf * xf, axis=-1, keepdims=True) + 1e-6)


def rotary(x, pos):
    half = x.shape[-1] // 2
    inv = jnp.float32(ROPE_THETA) ** (-jnp.arange(half, dtype=jnp.float32) / half)
    ang = pos.astype(jnp.float32)[:, None] * inv[None, :]
    cos = jnp.cos(ang)[:, None, :]
    sin = jnp.sin(ang)[:, None, :]
    xf = x.astype(jnp.float32)
    x1, x2 = xf[..., :half], xf[..., half:]
    return jnp.concatenate([x1 * cos - x2 * sin, x2 * cos + x1 * sin], axis=-1).astype(x.dtype)


def causal_dwconv(x, past, w):
    T = x.shape[1]
    width = w.shape[0]
    xp = jnp.concatenate([past.astype(x.dtype), x], axis=1)
    out = w[0] * xp[:, :T]
    for j in range(1, width):
        out = out + w[j] * xp[:, j:j + T]
    return out, xp[:, T:]


def masked_softmax(s, mask, axes):
    s = jnp.where(mask, s.astype(jnp.float32), -jnp.inf)
    m = jnp.max(s, axis=axes, keepdims=True)
    m = jnp.where(jnp.isfinite(m), m, 0.0)
    e = jnp.where(mask, jnp.exp(s - m), 0.0)
    return e / jnp.maximum(jnp.sum(e, axis=axes, keepdims=True), 1e-30)


def gated_delta_chunked(q, k, v, g, beta, s0):
    B, T, H, dk = q.shape
    dv = v.shape[-1]
    C = min(GDN_CHUNK, T)
    n_ch = -(-T // C)
    pad = n_ch * C - T

    def prep(a):
        a = jnp.pad(a.astype(jnp.float32), [(0, 0), (0, pad)] + [(0, 0)] * (a.ndim - 2))
        a = a.reshape((B, n_ch, C) + a.shape[2:])
        return jnp.moveaxis(a, 3, 1)

    q, k, v, g, beta = prep(q), prep(k), prep(v), prep(g), prep(beta)
    gc = jnp.cumsum(g, axis=-1)
    idx = jnp.arange(C)
    incl = idx[:, None] >= idx[None, :]
    strict = idx[:, None] > idx[None, :]
    diff = gc[..., :, None] - gc[..., None, :]
    decay = jnp.where(incl, jnp.exp(jnp.where(incl, diff, 0.0)), 0.0)
    kb = k * beta[..., None]
    a_mat = jnp.where(strict, jnp.einsum('bhnck,bhnjk->bhncj', kb, k) * decay, 0.0)
    rhs = jnp.concatenate([v * beta[..., None], kb * jnp.exp(gc)[..., None]], axis=-1)
    sol = lax.linalg.triangular_solve(a_mat + jnp.eye(C, dtype=jnp.float32), rhs,
                                      left_side=True, lower=True, unit_diagonal=True)
    u, w = sol[..., :dv], sol[..., dv:]
    qk = jnp.einsum('bhnck,bhnjk->bhncj', q, k) * decay

    def step(S, xs):
        qn, kn, un, wn, gn, qkn = xs
        v_new = un - jnp.einsum('bhck,bhkv->bhcv', wn, S)
        o = (jnp.einsum('bhck,bhkv->bhcv', qn * jnp.exp(gn)[..., None], S)
             + jnp.einsum('bhcj,bhjv->bhcv', qkn, v_new))
        g_last = gn[..., -1:]
        S = S * jnp.exp(g_last)[..., None] + jnp.einsum(
            'bhck,bhcv->bhkv', kn * jnp.exp(g_last - gn)[..., None], v_new)
        return S, o

    xs = tuple(jnp.moveaxis(a, 2, 0) for a in (q, k, u, w, gc, qk))
    S, o = lax.scan(step, s0.astype(jnp.float32), xs)
    o = jnp.moveaxis(o, 0, 2).reshape(B, H, n_ch * C, dv)[:, :, :T].transpose(0, 2, 1, 3)
    return o, S


def gdn_mixer(h, conv_past, s0, w_in, conv_w, A_log, dt_bias, norm_w, w_out):
    B, T, _ = h.shape
    hz = GDN_HEADS * GDN_DV
    hk = GDN_HEADS * GDN_DK
    proj = h @ w_in
    qkv = proj[..., :GDN_QKV]
    z = proj[..., GDN_QKV:GDN_QKV + hz]
    a = proj[..., GDN_QKV + hz:GDN_QKV + hz + GDN_HEADS]
    b = proj[..., GDN_QKV + hz + GDN_HEADS:]
    qkv, conv_new = causal_dwconv(qkv, conv_past, conv_w)
    qkv = jax.nn.silu(qkv)
    q = l2norm(qkv[..., :hk].reshape(B, T, GDN_HEADS, GDN_DK)) * (GDN_DK ** -0.5)
    k = l2norm(qkv[..., hk:2 * hk].reshape(B, T, GDN_HEADS, GDN_DK))
    v = qkv[..., 2 * hk:].reshape(B, T, GDN_HEADS, GDN_DV)
    beta = jax.nn.sigmoid(b.astype(jnp.float32))
    g = -jnp.exp(A_log.astype(jnp.float32)) * jax.nn.softplus(a.astype(jnp.float32) + dt_bias.astype(jnp.float32))
    o, s_new = gated_delta_chunked(q, k, v, g, beta, s0)
    o = o * lax.rsqrt(jnp.mean(o * o, axis=-1, keepdims=True) + RMS_EPS) * norm_w.astype(jnp.float32)
    o = (o * jax.nn.silu(z.reshape(B, T, GDN_HEADS, GDN_DV).astype(jnp.float32))).astype(h.dtype)
    return o.reshape(B, T, hz) @ w_out, conv_new, s_new.astype(s0.dtype)


def conv_ffn(h, past, w_gate, w_up, conv_w, conv_b, w_down):
    a, new_past = causal_dwconv(h @ w_gate, past, conv_w)
    return (jax.nn.silu(a + conv_b) * (h @ w_up)) @ w_down, new_past


def shared_kv_rows(stream, pos, norm_kv, kv_w):
    B, T, _ = stream.shape
    kv = (rmsnorm(stream, norm_kv) @ kv_w).reshape(B, T, 6, NSA_KV_GROUPS, NSA_HEAD_DIM)
    cmp_rows = kv[:, :, 0:2]
    sel_rows = jnp.stack([rotary(kv[:, :, 2], pos), kv[:, :, 3]], axis=2)
    win_rows = jnp.stack([rotary(kv[:, :, 4], pos), kv[:, :, 5]], axis=2)
    return cmp_rows, sel_rows, win_rows


def compress_phi(seg, pe, w1, w2):
    pe2 = pe.reshape(2, CMP_STRIDE, 1, NSA_HEAD_DIM)
    w1r = w1.reshape(2, CMP_STRIDE, NSA_HEAD_DIM, CMP_HIDDEN)
    lo = jnp.einsum('bstgd,tdh->bsgh', seg + pe2[0], w1r[0])
    hi = jnp.einsum('bstgd,tdh->bsgh', seg + pe2[1], w1r[1])
    return jax.nn.silu(lo[:, :-1] + hi[:, 1:]) @ w2


def compress_blocks(rows, pe_k, pe_v, w1_k, w2_k, w1_v, w2_v):
    B, T = rows.shape[:2]
    n_seg = -(-T // CMP_STRIDE)
    rows = jnp.pad(rows, ((0, 0), (0, n_seg * CMP_STRIDE - T), (0, 0), (0, 0), (0, 0)))
    seg = rows.reshape(B, n_seg, CMP_STRIDE, 2, NSA_KV_GROUPS, NSA_HEAD_DIM)
    kc = compress_phi(seg[:, :, :, 0], pe_k, w1_k, w2_k)
    vc = compress_phi(seg[:, :, :, 1], pe_v, w1_v, w2_v)
    c_end = jnp.arange(n_seg - 1, dtype=jnp.int32) * CMP_STRIDE + (CMP_BLOCK - 1)
    return kc, vc, c_end


def select_blocks(rows):
    B, T = rows.shape[:2]
    n_sel = -(-T // SEL_BLOCK)
    rows = jnp.pad(rows, ((0, 0), (0, n_sel * SEL_BLOCK - T), (0, 0), (0, 0), (0, 0)))
    blk = rows.reshape(B, n_sel, SEL_BLOCK, 2, NSA_KV_GROUPS, NSA_HEAD_DIM).transpose(3, 0, 4, 1, 2, 5)
    return blk[0], blk[1], jnp.arange(n_sel, dtype=jnp.int32) * SEL_BLOCK


def block_cover(n_cmp, n_sel):
    i = jnp.arange(n_cmp)[:, None] * CMP_STRIDE
    j = jnp.arange(n_sel)[None, :] * SEL_BLOCK
    return ((i < j + SEL_BLOCK) & (i + CMP_BLOCK > j)).astype(jnp.float32)


def gather_pages(pool, page_table):
    rows = pool[page_table]
    return rows.reshape((page_table.shape[0], page_table.shape[1] * pool.shape[1]) + pool.shape[2:])


def nsa_branches(q_raw, q_rot, gates, qpos, kc, vc, c_end, cover, ks_blk, vs_blk, s_start, kw, vw, wpos):
    B, Tq, H, hd = q_raw.shape
    scale = hd ** -0.5
    qr = q_raw.reshape(B, Tq, NSA_KV_GROUPS, NSA_HPG, hd)
    qo = q_rot.reshape(B, Tq, NSA_KV_GROUPS, NSA_HPG, hd)
    gt = gates.reshape(B, Tq, NSA_KV_GROUPS, NSA_HPG, 3)
    s_c = jnp.einsum('bqghd,bngd->bqghn', qr, kc) * scale
    m_c = (c_end[None, :] <= qpos[:, None])[None, :, None, None, :]
    p_c = masked_softmax(s_c, m_c, (-1,))
    o_c = jnp.einsum('bqghn,bngd->bqghd', p_c, vc)
    n_sel = ks_blk.shape[2]
    imp = jnp.einsum('bqghn,nj->bqgj', p_c, cover)
    blk = jnp.arange(n_sel)
    cur = qpos // SEL_BLOCK
    valid = s_start[None, :] <= qpos[:, None]
    force = (blk[None, :] == 0) | (blk[None, :] >= cur[:, None] - (N_LOCAL - 1))
    score = jnp.where(valid[None, :, None, :], jnp.where(force[None, :, None, :], SEL_FORCE, imp), -1.0)
    _, idx = lax.top_k(score, min(N_SELECT, n_sel))
    bi = jnp.arange(B)[:, None, None, None]
    gi = jnp.arange(NSA_KV_GROUPS)[None, None, :, None]
    ks = ks_blk[bi, gi, idx]
    vs = vs_blk[bi, gi, idx]
    tok = idx[..., None] * SEL_BLOCK + jnp.arange(SEL_BLOCK)
    m_s = (tok <= qpos[None, :, None, None, None])[:, :, :, None]
    s_s = jnp.einsum('bqghd,bqgnld->bqghnl', qo, ks) * scale
    p_s = masked_softmax(s_s, m_s, (-2, -1))
    o_s = jnp.einsum('bqghnl,bqgnld->bqghd', p_s, vs)
    s_w = jnp.einsum('bqghd,bkgd->bqghk', qo, kw) * scale
    m_w = ((wpos[None, :] <= qpos[:, None]) & (wpos[None, :] > qpos[:, None] - WINDOW)
           & (wpos[None, :] >= 0))[None, :, None, None, :]
    p_w = masked_softmax(s_w, m_w, (-1,))
    o_w = jnp.einsum('bqghk,bkgd->bqghd', p_w, vw)
    o = gt[..., 0:1] * o_c + gt[..., 1:2] * o_s + gt[..., 2:3] * o_w
    return o.reshape(B, Tq, H, hd).astype(q_raw.dtype)


def nsa_mixer(h, pos, w_q, w_out, ctx, win_ctx, prompt):
    B, T, _ = h.shape
    hq = NSA_HEADS * NSA_HEAD_DIM
    proj = h @ w_q
    q = proj[..., :hq].reshape(B, T, NSA_HEADS, NSA_HEAD_DIM)
    gates = jax.nn.sigmoid(proj[..., hq:].astype(jnp.float32)).reshape(B, T, NSA_HEADS, 3)
    q_rot = rotary(q, pos)
    win_k, win_v, win_pos = win_ctx
    if prompt:
        n_blk = T // Q_BLOCK
        span = WINDOW + Q_BLOCK

        def body(args):
            qr, qo, gt, qp, i = args
            start = i * Q_BLOCK
            kw = lax.dynamic_slice_in_dim(win_k, start, span, axis=1)
            vw = lax.dynamic_slice_in_dim(win_v, start, span, axis=1)
            wp = lax.dynamic_slice_in_dim(win_pos, start, span, axis=0)
            return nsa_branches(qr, qo, gt, qp, *ctx, kw, vw, wp)

        def blocks(a):
            return jnp.moveaxis(a.reshape((B, n_blk, Q_BLOCK) + a.shape[2:]), 1, 0)

        o = lax.map(body, (blocks(q), blocks(q_rot), blocks(gates), pos.reshape(n_blk, Q_BLOCK),
                           jnp.arange(n_blk, dtype=jnp.int32)))
        o = jnp.moveaxis(o, 0, 1).reshape(B, T, NSA_HEADS, NSA_HEAD_DIM)
    else:
        o = nsa_branches(q, q_rot, gates, pos, *ctx, win_k, win_v, win_pos)
    return o.reshape(B, T, hq) @ w_out


def setup_inputs(seed: int = 0) -> dict:
    key = jax.random.key(seed)
    ks = iter(jax.random.split(key, 48))

    def nrm(shape, scale):
        return jax.random.normal(next(ks), shape, jnp.float32) * scale

    n_pages = PAST_LEN // PAGE_SIZE
    used = DEC_BATCH * n_pages
    n_pool = used + max(1, used // 4)
    win_buf = min(WINDOW, PAST_LEN)
    G, HD = NSA_KV_GROUPS, NSA_HEAD_DIM
    page_table = jax.random.permutation(next(ks), n_pool)[:used].reshape(DEC_BATCH, n_pages).astype(jnp.int32)
    A_log = jnp.log(jax.random.uniform(next(ks), (N_A_LAYERS, GDN_HEADS), jnp.float32, 1.0, 16.0))
    dt = jnp.exp(jax.random.uniform(next(ks), (N_A_LAYERS, GDN_HEADS), jnp.float32,
                                    float(np.log(1e-3)), float(np.log(1e-1))))
    dt_bias = dt + jnp.log(-jnp.expm1(-dt))
    return {
        'x_prompt': nrm((BATCH, SEQ, D_MODEL), 1.0),
        'x_sample': nrm((DEC_BATCH, DEC_SEQ, D_MODEL), 1.0),
        'state_gdn': nrm((N_A_LAYERS, DEC_BATCH, GDN_HEADS, GDN_DK, GDN_DV), 0.1),
        'state_gdn_conv': nrm((N_A_LAYERS, DEC_BATCH, GDN_CONV - 1, GDN_QKV), 1.0),
        'state_ffn_conv': nrm((DEPTH, DEC_BATCH, FFN_CONV - 1, D_FF), 1.0),
        'cache_cmp_kv': nrm((n_pool, PAGE_SIZE, 2, G, HD), 1.0),
        'cache_sel_kv': nrm((n_pool, PAGE_SIZE, 2, G, HD), 1.0),
        'cache_win_kv': nrm((DEC_BATCH, win_buf, 2, G, HD), 1.0),
        'page_table': page_table,
        'norm_mixer': 1.0 + nrm((DEPTH, D_MODEL), 0.02),
        'norm_ffn': 1.0 + nrm((DEPTH, D_MODEL), 0.02),
        'norm_kv': 1.0 + nrm((D_MODEL,), 0.02),
        'norm_final': 1.0 + nrm((D_MODEL,), 0.02),
        'gdn_w_in': nrm((N_A_LAYERS, D_MODEL, GDN_QKV + GDN_HEADS * GDN_DV + 2 * GDN_HEADS), D_MODEL ** -0.5),
        'gdn_conv_w': nrm((N_A_LAYERS, GDN_CONV, GDN_QKV), 0.5),
        'gdn_A_log': A_log,
        'gdn_dt_bias': dt_bias,
        'gdn_norm_w': 1.0 + nrm((N_A_LAYERS, GDN_DV), 0.02),
        'gdn_w_out': nrm((N_A_LAYERS, GDN_HEADS * GDN_DV, D_MODEL), (GDN_HEADS * GDN_DV) ** -0.5),
        'nsa_w_q': nrm((N_B_LAYERS, D_MODEL, NSA_HEADS * HD + 3 * NSA_HEADS), D_MODEL ** -0.5),
        'nsa_w_out': nrm((N_B_LAYERS, NSA_HEADS * HD, D_MODEL), (NSA_HEADS * HD) ** -0.5),
        'kv_w': nrm((D_MODEL, 6 * G * HD), D_MODEL ** -0.5),
        'cmp_pe_k': nrm((CMP_BLOCK, HD), 0.1),
        'cmp_pe_v': nrm((CMP_BLOCK, HD), 0.1),
        'cmp_w1_k': nrm((CMP_BLOCK * HD, CMP_HIDDEN), (CMP_BLOCK * HD) ** -0.5),
        'cmp_w2_k': nrm((CMP_HIDDEN, HD), CMP_HIDDEN ** -0.5),
        'cmp_w1_v': nrm((CMP_BLOCK * HD, CMP_HIDDEN), (CMP_BLOCK * HD) ** -0.5),
        'cmp_w2_v': nrm((CMP_HIDDEN, HD), CMP_HIDDEN ** -0.5),
        'ffn_w_gate': nrm((DEPTH, D_MODEL, D_FF), D_MODEL ** -0.5),
        'ffn_w_up': nrm((DEPTH, D_MODEL, D_FF), D_MODEL ** -0.5),
        'ffn_conv_w': nrm((DEPTH, FFN_CONV, D_FF), 0.5),
        'ffn_conv_b': nrm((DEPTH, D_FF), 0.01),
        'ffn_w_down': nrm((DEPTH, D_FF, D_MODEL), D_FF ** -0.5),
    }


def reference(x_prompt, x_sample, state_gdn, state_gdn_conv, state_ffn_conv, cache_cmp_kv, cache_sel_kv,
              cache_win_kv, page_table, norm_mixer, norm_ffn, norm_kv, norm_final, gdn_w_in, gdn_conv_w,
              gdn_A_log, gdn_dt_bias, gdn_norm_w, gdn_w_out, nsa_w_q, nsa_w_out, kv_w, cmp_pe_k, cmp_pe_v,
              cmp_w1_k, cmp_w2_k, cmp_w1_v, cmp_w2_v, ffn_w_gate, ffn_w_up, ffn_conv_w, ffn_conv_b, ffn_w_down):

    def build_shared(stream, pos, past):
        T = stream.shape[1]
        cmp_rows, sel_rows, win_rows = shared_kv_rows(stream, pos, norm_kv, kv_w)
        if past is None:
            cmp_all, sel_all = cmp_rows, sel_rows
            win_all = jnp.pad(win_rows, ((0, 0), (WINDOW, 0), (0, 0), (0, 0), (0, 0)))
            win_pos = jnp.arange(-WINDOW, T, dtype=jnp.int32)
            new_win = win_rows[:, T - min(WINDOW, T):]
        else:
            p_cmp, p_sel, p_win, past_len = past
            cmp_all = jnp.concatenate([p_cmp.astype(cmp_rows.dtype), cmp_rows], axis=1)
            sel_all = jnp.concatenate([p_sel.astype(sel_rows.dtype), sel_rows], axis=1)
            wb = p_win.shape[1]
            win_all = jnp.concatenate([p_win.astype(win_rows.dtype), win_rows], axis=1)
            win_pos = jnp.concatenate([past_len - wb + jnp.arange(wb, dtype=jnp.int32), pos])
            new_win = win_all[:, win_all.shape[1] - wb:]
        kc, vc, c_end = compress_blocks(cmp_all, cmp_pe_k, cmp_pe_v, cmp_w1_k, cmp_w2_k, cmp_w1_v, cmp_w2_v)
        ks_blk, vs_blk, s_start = select_blocks(sel_all)
        cover = block_cover(kc.shape[1], ks_blk.shape[2])
        ctx = (kc, vc, c_end, cover, ks_blk, vs_blk, s_start)
        return ctx, (win_all[:, :, 0], win_all[:, :, 1], win_pos), (cmp_rows, sel_rows, new_win)

    def run(x, pos, s0_gdn, c0_gdn, c0_ffn, past):
        gdn_s, gdn_c, ffn_c = [], [], []
        ctx, win_ctx, rows = None, None, None
        for layer in range(DEPTH):
            h = rmsnorm(x, norm_mixer[layer])
            if layer < N_A_LAYERS:
                mix, c_new, s_new = gdn_mixer(h, c0_gdn[layer], s0_gdn[layer], gdn_w_in[layer], gdn_conv_w[layer],
                                              gdn_A_log[layer], gdn_dt_bias[layer], gdn_norm_w[layer],
                                              gdn_w_out[layer])
                gdn_s.append(s_new)
                gdn_c.append(c_new)
            else:
                if ctx is None:
                    ctx, win_ctx, rows = build_shared(x, pos, past)
                j = layer - N_A_LAYERS
                mix = nsa_mixer(h, pos, nsa_w_q[j], nsa_w_out[j], ctx, win_ctx, past is None)
            x = x + mix
            f, fc_new = conv_ffn(rmsnorm(x, norm_ffn[layer]), c0_ffn[layer], ffn_w_gate[layer], ffn_w_up[layer],
                                 ffn_conv_w[layer], ffn_conv_b[layer], ffn_w_down[layer])
            ffn_c.append(fc_new)
            x = x + f
        y = rmsnorm(x, norm_final)
        return y, jnp.stack(gdn_s), jnp.stack(gdn_c), jnp.stack(ffn_c), rows[0], rows[1], rows[2]

    b_p, t_p = x_prompt.shape[:2]
    pos_p = jnp.arange(t_p, dtype=jnp.int32)
    y_prompt, gdn_state_p, gdn_conv_p, ffn_conv_p, cmp_new_p, sel_new_p, win_new_p = run(
        x_prompt, pos_p,
        jnp.zeros((N_A_LAYERS, b_p, GDN_HEADS, GDN_DK, GDN_DV), state_gdn.dtype),
        jnp.zeros((N_A_LAYERS, b_p, GDN_CONV - 1, GDN_QKV), x_prompt.dtype),
        jnp.zeros((DEPTH, b_p, FFN_CONV - 1, D_FF), x_prompt.dtype),
        None)

    past_len = page_table.shape[1] * cache_sel_kv.shape[1]
    pos_s = past_len + jnp.arange(x_sample.shape[1], dtype=jnp.int32)
    past = (gather_pages(cache_cmp_kv, page_table), gather_pages(cache_sel_kv, page_table), cache_win_kv, past_len)
    y_sample, gdn_state_s, gdn_conv_s, ffn_conv_s, cmp_new_s, sel_new_s, win_new_s = run(
        x_sample, pos_s, state_gdn, state_gdn_conv, state_ffn_conv, past)

    return (y_prompt, y_sample, gdn_state_p, gdn_conv_p, ffn_conv_p, cmp_new_p, sel_new_p, win_new_p,
            gdn_state_s, gdn_conv_s, ffn_conv_s, cmp_new_s, sel_new_s, win_new_s)
```

```python
import functools

import jax
import jax.numpy as jnp
import numpy as np
from jax import lax
from jax.experimental import pallas as pl
from jax.experimental.pallas import tpu as pltpu

F32 = jnp.float32
BF16 = jnp.bfloat16

RMS_EPS = 1e-6
ROPE_THETA = 10000.0
HEAD_DIM = 128
GDN_HEADS = 16
GDN_CONV = 4
GDN_CHUNK = 128
NSA_HEADS = 16
NSA_GROUPS = 4
NSA_HPG = NSA_HEADS // NSA_GROUPS
CMP_STRIDE = 16
CMP_BLOCK = 32
SEL_BLOCK = 64
N_SELECT = 16
WINDOW = 512
Q_BLOCK = 128
FFN_CONV = 3
NEG_BIG = -1e30

VMEM_LIMIT = 52 * 1024 * 1024


def _cparams(*sem):
    return pltpu.CompilerParams(dimension_semantics=sem, vmem_limit_bytes=VMEM_LIMIT)


def _bf(x):
    return x.astype(BF16)


def _dot(a, b):
    return jnp.dot(_bf(a), _bf(b), preferred_element_type=F32)


def _dot_nt(a, b):
    return lax.dot_general(_bf(a), _bf(b), (((1,), (1,)), ((), ())), preferred_element_type=F32)


def _dot_tn(a, b):
    return lax.dot_general(_bf(a), _bf(b), (((0,), (0,)), ((), ())), preferred_element_type=F32)


def _sigmoid(x):
    return 1.0 / (1.0 + jnp.exp(-x))


def _silu(x):
    return x * _sigmoid(x)


def _rms(x, w):
    return x * lax.rsqrt(jnp.mean(x * x, axis=-1, keepdims=True) + RMS_EPS) * w


def _rms_mm_kernel(x_ref, nw_ref, w_ref, o_ref, h_ref):
    @pl.when(pl.program_id(1) == 0)
    def _():
        h_ref[...] = _bf(_rms(x_ref[...], nw_ref[...]))

    o_ref[...] = jnp.dot(h_ref[...], _bf(w_ref[...]), preferred_element_type=F32).astype(o_ref.dtype)


def rms_matmul(x, nw, w, n_out, tn, tm=1024, out_dtype=F32):
    m, k = x.shape
    tm = min(tm, m)
    return pl.pallas_call(
        _rms_mm_kernel,
        name="rms_mm",
        grid=(m // tm, n_out // tn),
        in_specs=[
            pl.BlockSpec((tm, k), lambda i, j: (i, 0)),
            pl.BlockSpec((1, k), lambda i, j: (0, 0)),
            pl.BlockSpec((k, tn), lambda i, j: (0, j)),
        ],
        out_specs=pl.BlockSpec((tm, tn), lambda i, j: (i, j)),
        out_shape=jax.ShapeDtypeStruct((m, n_out), out_dtype),
        scratch_shapes=[pltpu.VMEM((tm, k), BF16)],
        compiler_params=_cparams("parallel", "arbitrary"),
    )(x, nw.reshape(1, k), w)


def _mm_res_kernel(a_ref, w_ref, r_ref, *rest, final_norm):
    if final_norm:
        nw_ref, o_ref, acc_ref = rest
    else:
        o_ref, acc_ref = rest
    kk = pl.program_id(1)

    @pl.when(kk == 0)
    def _():
        acc_ref[...] = r_ref[...]

    acc_ref[...] += jnp.dot(a_ref[...], _bf(w_ref[...]), preferred_element_type=F32)

    @pl.when(kk == pl.num_programs(1) - 1)
    def _():
        if final_norm:
            o_ref[...] = _rms(acc_ref[...], nw_ref[...])
        else:
            o_ref[...] = acc_ref[...]


def matmul_res(a, w, res, final_nw=None, tm=512, tk=512):
    m, k = a.shape
    n = w.shape[1]
    tm = min(tm, m)
    in_specs = [
        pl.BlockSpec((tm, tk), lambda i, kk: (i, kk)),
        pl.BlockSpec((tk, n), lambda i, kk: (kk, 0)),
        pl.BlockSpec((tm, n), lambda i, kk: (i, 0)),
    ]
    args = [a, w, res]
    if final_nw is not None:
        in_specs.append(pl.BlockSpec((1, n), lambda i, kk: (0, 0)))
        args.append(final_nw.reshape(1, n))
    return pl.pallas_call(
        functools.partial(_mm_res_kernel, final_norm=final_nw is not None),
        name="mm_res",
        grid=(m // tm, k // tk),
        in_specs=in_specs,
        out_specs=pl.BlockSpec((tm, n), lambda i, kk: (i, 0)),
        out_shape=jax.ShapeDtypeStruct((m, n), F32),
        scratch_shapes=[pltpu.VMEM((tm, n), F32)],
        compiler_params=_cparams("parallel", "arbitrary"),
    )(*args)


def _ffn_up_seq_kernel(x_ref, nw_ref, wg_ref, wu_ref, cw_ref, cb_ref, act_ref, st_ref, h_ref, carry_ref, buf_ref):
    i, j = pl.program_id(0), pl.program_id(1)
    tm = x_ref.shape[0]

    @pl.when(j == 0)
    def _():
        h_ref[...] = _bf(_rms(x_ref[...], nw_ref[...]))

    @pl.when(i == 0)
    def _():
        carry_ref[j] = jnp.zeros(carry_ref.shape[1:], F32)

    h = h_ref[...]
    g = jnp.dot(h, _bf(wg_ref[...]), preferred_element_type=F32)
    u = jnp.dot(h, _bf(wu_ref[...]), preferred_element_type=F32)
    buf_ref[0:8, :] = carry_ref[j]
    buf_ref[8:8 + tm, :] = g
    cw = cw_ref[...]
    a = cw[0:1] * buf_ref[6:6 + tm, :] + cw[1:2] * buf_ref[7:7 + tm, :] + cw[2:3] * g
    act_ref[...] = _bf(_silu(a + cb_ref[...]) * u)
    carry_ref[j] = g[tm - 8:tm]
    st_ref[0] = g[tm - 8:tm]


def ffn_up_seq(x, nw, wg, wu, cw, cb, tm=1024, tn=512):
    m, k = x.shape
    f = wg.shape[1]
    nb = f // tn
    act, st = pl.pallas_call(
        _ffn_up_seq_kernel,
        name="ffn_up_seq",
        grid=(m // tm, nb),
        in_specs=[
            pl.BlockSpec((tm, k), lambda i, j: (i, 0)),
            pl.BlockSpec((1, k), lambda i, j: (0, 0)),
            pl.BlockSpec((k, tn), lambda i, j: (0, j)),
            pl.BlockSpec((k, tn), lambda i, j: (0, j)),
            pl.BlockSpec((FFN_CONV, tn), lambda i, j: (0, j)),
            pl.BlockSpec((1, tn), lambda i, j: (0, j)),
        ],
        out_specs=[
            pl.BlockSpec((tm, tn), lambda i, j: (i, j)),
            pl.BlockSpec((1, 8, tn), lambda i, j: (i, 0, j)),
        ],
        out_shape=[
            jax.ShapeDtypeStruct((m, f), BF16),
            jax.ShapeDtypeStruct((m // tm, 8, f), F32),
        ],
        scratch_shapes=[
            pltpu.VMEM((tm, k), BF16),
            pltpu.VMEM((nb, 8, tn), F32),
            pltpu.VMEM((tm + 8, tn), F32),
        ],
        compiler_params=_cparams("arbitrary", "arbitrary"),
    )(x, nw.reshape(1, k), wg, wu, cw, cb.reshape(1, f))
    return act, st[-1, 8 - (FFN_CONV - 1):]


def _ffn_up_step_kernel(x_ref, nw_ref, wg_ref, wu_ref, cw_ref, cb_ref, p0_ref, p1_ref, act_ref, g_ref, h_ref):
    @pl.when(pl.program_id(1) == 0)
    def _():
        h_ref[...] = _bf(_rms(x_ref[...], nw_ref[...]))

    h = h_ref[...]
    g = jnp.dot(h, _bf(wg_ref[...]), preferred_element_type=F32)
    u = jnp.dot(h, _bf(wu_ref[...]), preferred_element_type=F32)
    cw = cw_ref[...]
    a = cw[0:1] * p0_ref[...] + cw[1:2] * p1_ref[...] + cw[2:3] * g
    act_ref[...] = _bf(_silu(a + cb_ref[...]) * u)
    g_ref[...] = g


def ffn_up_step(x, nw, wg, wu, cw, cb, past, tn=512):
    m, k = x.shape
    f = wg.shape[1]
    nb = f // tn
    past2 = past.reshape(m, (FFN_CONV - 1) * f)
    act, g = pl.pallas_call(
        _ffn_up_step_kernel,
        name="ffn_up_step",
        grid=(1, nb),
        in_specs=[
            pl.BlockSpec((m, k), lambda i, j: (0, 0)),
            pl.BlockSpec((1, k), lambda i, j: (0, 0)),
            pl.BlockSpec((k, tn), lambda i, j: (0, j)),
            pl.BlockSpec((k, tn), lambda i, j: (0, j)),
            pl.BlockSpec((FFN_CONV, tn), lambda i, j: (0, j)),
            pl.BlockSpec((1, tn), lambda i, j: (0, j)),
            pl.BlockSpec((m, tn), lambda i, j: (0, j)),
            pl.BlockSpec((m, tn), lambda i, j: (0, j + nb)),
        ],
        out_specs=[
            pl.BlockSpec((m, tn), lambda i, j: (0, j)),
            pl.BlockSpec((m, tn), lambda i, j: (0, j)),
        ],
        out_shape=[
            jax.ShapeDtypeStruct((m, f), BF16),
            jax.ShapeDtypeStruct((m, f), F32),
        ],
        scratch_shapes=[pltpu.VMEM((m, k), BF16)],
        compiler_params=_cparams("arbitrary", "arbitrary"),
    )(x, nw.reshape(1, k), wg, wu, cw, cb.reshape(1, f), past2, past2)
    return act, jnp.stack([past[:, 1], g], axis=1)


GDN_HPS = 4


def _cumsum_rows(x):
    row = lax.broadcasted_iota(jnp.int32, x.shape, 0)
    s = 1
    while s < x.shape[0]:
        x = x + jnp.where(row >= s, pltpu.roll(x, s, 0), 0.0)
        s *= 2
    return x


def _softplus(x):
    return jnp.maximum(x, 0.0) + jnp.log1p(jnp.exp(-jnp.abs(x)))


def _l2norm(x):
    return x * lax.rsqrt(jnp.sum(x * x, axis=-1, keepdims=True) + 1e-6)


def _gdn_seq_kernel(qp_ref, kp_ref, vp_ref, qh_ref, kh_ref, vh_ref, qpast_ref, kpast_ref, vpast_ref,
                    qw_ref, kw_ref, vw_ref, z_ref, ab_ref, alog_ref, dtb_ref, nw_ref, s0_ref,
                    o_ref, sout_ref, s_scr, buf_ref):
    hg, c = pl.program_id(0), pl.program_id(1)
    C = GDN_CHUNK
    D = HEAD_DIM

    @pl.when(c == 0)
    def _():
        s_scr[...] = s0_ref[...]

    def conv(p_ref, h_ref, past_ref, w_ref):
        buf_ref[0:8, :] = jnp.where(c == 0, past_ref[...], h_ref[...])
        buf_ref[8:8 + C, :] = p_ref[...]
        w = w_ref[...]
        y = (w[0:1] * buf_ref[5:5 + C, :] + w[1:2] * buf_ref[6:6 + C, :]
             + w[2:3] * buf_ref[7:7 + C, :] + w[3:4] * buf_ref[8:8 + C, :])
        return _silu(y)

    qc = conv(qp_ref, qh_ref, qpast_ref, qw_ref)
    kc = conv(kp_ref, kh_ref, kpast_ref, kw_ref)
    vc = conv(vp_ref, vh_ref, vpast_ref, vw_ref)

    ab = ab_ref[...]
    lane = lax.broadcasted_iota(jnp.int32, (C, 128), 1)
    g_all = -jnp.exp(alog_ref[...]) * _softplus(ab + dtb_ref[...])
    gc_all = _cumsum_rows(g_all)
    beta_all = _sigmoid(ab)

    row = lax.broadcasted_iota(jnp.int32, (C, C), 0)
    col = lax.broadcasted_iota(jnp.int32, (C, C), 1)
    eye = (row == col).astype(F32)
    nw = nw_ref[...]

    for j in range(GDN_HPS):
        head = hg * GDN_HPS + j
        gcol = jnp.sum(jnp.where(lane == head, gc_all, 0.0), axis=1, keepdims=True)
        bcol = jnp.sum(jnp.where(lane == head + GDN_HEADS, beta_all, 0.0), axis=1, keepdims=True)
        colb = jnp.broadcast_to(gcol, (C, C))
        dec = jnp.exp(jnp.minimum(colb - colb.T, 0.0))
        q = _l2norm(qc[:, j * D:(j + 1) * D]) * (D ** -0.5)
        k = _l2norm(kc[:, j * D:(j + 1) * D])
        v = vc[:, j * D:(j + 1) * D]
        kb = k * bcol
        a_dec = _dot_nt(kb, k) * dec
        qk = jnp.where(row >= col, _dot_nt(q, k) * dec, 0.0)
        tm = eye - jnp.where(((row ^ col) == 1) & ((row & 1) == 1), a_dec, 0.0)
        s = 2
        while s < C:
            msk = ((row // (2 * s)) == (col // (2 * s))) & ((row & s) != 0) & ((col & s) == 0)
            tm = tm - _dot(tm, _dot(jnp.where(msk, a_dec, 0.0), tm))
            s *= 2
        egc = jnp.exp(gcol)
        uw = _dot(tm, jnp.concatenate([v * bcol, kb * egc], axis=1))
        st = s_scr[j]
        r = _dot(jnp.concatenate([uw[:, D:], q * egc], axis=0), st)
        v_new = uw[:, :D] - r[:C]
        o = r[C:] + _dot(qk, v_new)
        g_last = colb[C - 1:C, :]
        kdec = k * jnp.exp(g_last[:, 0:1] - gcol)
        s_scr[j] = st * jnp.exp(g_last) + _dot_tn(kdec, v_new)
        o = _rms(o, nw) * _silu(z_ref[:, j * D:(j + 1) * D])
        o_ref[:, j * D:(j + 1) * D] = _bf(o)

    @pl.when(c == pl.num_programs(1) - 1)
    def _():
        sout_ref[...] = s_scr[...]


def gdn_seq(proj, ab, conv_w, a_log, dt_bias, norm_w, past8, s0):
    t = proj.shape[0]
    C, W = GDN_CHUNK, GDN_HPS * HEAD_DIM
    nq = GDN_HEADS * HEAD_DIM // W
    pad16 = lambda a: jnp.pad(a.reshape(1, GDN_HEADS), ((0, 0), (0, 128 - GDN_HEADS)))

    def rows(off):
        return pl.BlockSpec((C, W), lambda hg, c: (c, off + hg))

    def halo(off):
        return pl.BlockSpec((8, W), lambda hg, c: (jnp.maximum(c * (C // 8) - 1, 0), off + hg))

    def fixed(rws, off):
        return pl.BlockSpec((rws, W), lambda hg, c: (0, off + hg))

    o, s_out = pl.pallas_call(
        _gdn_seq_kernel,
        name="gdn_seq",
        grid=(nq, t // C),
        in_specs=[rows(0), rows(nq), rows(2 * nq), halo(0), halo(nq), halo(2 * nq),
                  fixed(8, 0), fixed(8, nq), fixed(8, 2 * nq),
                  fixed(GDN_CONV, 0), fixed(GDN_CONV, nq), fixed(GDN_CONV, 2 * nq),
                  rows(3 * nq),
                  pl.BlockSpec((C, 128), lambda hg, c: (c, 0)),
                  pl.BlockSpec((1, 128), lambda hg, c: (0, 0)),
                  pl.BlockSpec((1, 128), lambda hg, c: (0, 0)),
                  pl.BlockSpec((1, HEAD_DIM), lambda hg, c: (0, 0)),
                  pl.BlockSpec((GDN_HPS, HEAD_DIM, HEAD_DIM), lambda hg, c: (hg, 0, 0))],
        out_specs=[pl.BlockSpec((C, W), lambda hg, c: (c, hg)),
                   pl.BlockSpec((GDN_HPS, HEAD_DIM, HEAD_DIM), lambda hg, c: (hg, 0, 0))],
        out_shape=[jax.ShapeDtypeStruct((t, GDN_HEADS * HEAD_DIM), BF16),
                   jax.ShapeDtypeStruct((GDN_HEADS, HEAD_DIM, HEAD_DIM), F32)],
        scratch_shapes=[pltpu.VMEM((GDN_HPS, HEAD_DIM, HEAD_DIM), F32),
                        pltpu.VMEM((C + 8, W), F32)],
        compiler_params=_cparams("parallel", "arbitrary"),
    )(proj, proj, proj, proj, proj, proj, past8, past8, past8,
      conv_w, conv_w, conv_w, proj, ab, pad16(a_log), pad16(dt_bias), norm_w.reshape(1, HEAD_DIM), s0)
    return o, s_out


def _pad_cols(w, n):
    return jnp.pad(w, ((0, 0), (0, n - w.shape[1])))


def gdn_layer_seq(x, nmix, w_in, conv_w, a_log, dt_bias, norm_w, w_out):
    nqkv = 3 * GDN_HEADS * HEAD_DIM
    nz = GDN_HEADS * HEAD_DIM
    proj = rms_matmul(x, nmix, w_in, nqkv + nz, 512)
    ab = rms_matmul(x, nmix, _pad_cols(w_in[:, nqkv + nz:], 128), 128, 128)
    past8 = jnp.zeros((8, nqkv), F32)
    s0 = jnp.zeros((GDN_HEADS, HEAD_DIM, HEAD_DIM), F32)
    o, s_out = gdn_seq(proj, ab, conv_w, a_log, dt_bias, norm_w, past8, s0)
    x1 = matmul_res(o, w_out, x)
    return x1, s_out, proj[-(GDN_CONV - 1):, :nqkv]


def ffn_layer_seq(x, nw, wg, wu, cw, cb, wd, final_nw=None):
    act, st = ffn_up_seq(x, nw, wg, wu, cw, cb)
    return matmul_res(act, wd, x, final_nw=final_nw), st


def ffn_layer_step(x, nw, wg, wu, cw, cb, wd, past, final_nw=None):
    act, st = ffn_up_step(x, nw, wg, wu, cw, cb, past)
    return matmul_res(act, wd, x, final_nw=final_nw), st


KV_SECTION = NSA_GROUPS * HEAD_DIM
KV_ROTARY_SECTIONS = (2, 4)


def _rope_tables(pos):
    half = HEAD_DIM // 2
    inv = jnp.float32(ROPE_THETA) ** (-jnp.arange(half, dtype=F32) / half)
    ang = pos.astype(F32)[:, None] * inv[None, :]
    cos, sin = jnp.cos(ang), jnp.sin(ang)
    return jnp.concatenate([cos, cos], axis=1), jnp.concatenate([-sin, sin], axis=1)


def _rope(x, cosf, sinf):
    return x * cosf + pltpu.roll(x, HEAD_DIM // 2, 1) * sinf


def _kv_rows_kernel(x_ref, nw_ref, w_ref, cos_ref, sin_ref, o_ref, ob_ref, h_ref):
    j = pl.program_id(1)

    @pl.when(j == 0)
    def _():
        h_ref[...] = _bf(_rms(x_ref[...], nw_ref[...]))

    y = jnp.dot(h_ref[...], _bf(w_ref[...]), preferred_element_type=F32)
    is_rot = (j == KV_ROTARY_SECTIONS[0]) | (j == KV_ROTARY_SECTIONS[1])

    @pl.when(is_rot)
    def _():
        cosf, sinf = cos_ref[...], sin_ref[...]
        for g in range(NSA_GROUPS):
            sl = slice(g * HEAD_DIM, (g + 1) * HEAD_DIM)
            yg = _rope(y[:, sl], cosf, sinf)
            o_ref[:, sl] = yg
            ob_ref[:, sl] = _bf(yg)

    @pl.when(jnp.logical_not(is_rot))
    def _():
        o_ref[...] = y
        ob_ref[...] = _bf(y)


def kv_rows(x, nw, w, cosf, sinf, tm=1024):
    m, k = x.shape
    n = w.shape[1]
    tm = min(tm, m)
    return pl.pallas_call(
        _kv_rows_kernel,
        name="kv_rows",
        grid=(m // tm, n // KV_SECTION),
        in_specs=[
            pl.BlockSpec((tm, k), lambda i, j: (i, 0)),
            pl.BlockSpec((1, k), lambda i, j: (0, 0)),
            pl.BlockSpec((k, KV_SECTION), lambda i, j: (0, j)),
            pl.BlockSpec((tm, HEAD_DIM), lambda i, j: (i, 0)),
            pl.BlockSpec((tm, HEAD_DIM), lambda i, j: (i, 0)),
        ],
        out_specs=[pl.BlockSpec((tm, KV_SECTION), lambda i, j: (i, j)),
                   pl.BlockSpec((tm, KV_SECTION), lambda i, j: (i, j))],
        out_shape=[jax.ShapeDtypeStruct((m, n), F32), jax.ShapeDtypeStruct((m, n), BF16)],
        scratch_shapes=[pltpu.VMEM((tm, k), BF16)],
        compiler_params=_cparams("parallel", "arbitrary"),
    )(x, nw.reshape(1, k), w, cosf, sinf)


def _compress_seq_kernel(rows_ref, pe_ref, w1_ref, w2_ref, o_ref, xs_ref, ps_ref):
    nseg = o_ref.shape[2]
    half = CMP_STRIDE * HEAD_DIM
    ps_ref[...] = jnp.zeros(ps_ref.shape, F32)
    for t in range(CMP_STRIDE):
        sl = slice(t * HEAD_DIM, (t + 1) * HEAD_DIM)
        xs_ref[:, sl] = _bf(rows_ref[pl.ds(t, nseg, stride=CMP_STRIDE), :])
        ps_ref[0:1, sl] = pe_ref[0, t:t + 1, :]
        ps_ref[8:9, sl] = pe_ref[0, CMP_STRIDE + t:CMP_STRIDE + t + 1, :]
    xs = xs_ref[...]
    w_lo = _bf(w1_ref[0, 0:half, :])
    w_hi = _bf(w1_ref[0, half:2 * half, :])
    lo = jnp.dot(xs, w_lo, preferred_element_type=F32) + _dot(ps_ref[0:8, :], w_lo)[0:1]
    hi = jnp.dot(xs, w_hi, preferred_element_type=F32) + _dot(ps_ref[8:16, :], w_hi)[0:1]
    hid = _silu(lo + pltpu.roll(hi, nseg - 1, 0))
    o_ref[0, 0] = _bf(_dot(hid, w2_ref[0]))


def compress_seq(kv, pe, w1, w2):
    t = kv.shape[0]
    nseg = t // CMP_STRIDE
    hid = w1.shape[2]
    return pl.pallas_call(
        _compress_seq_kernel,
        name="compress_seq",
        grid=(2, NSA_GROUPS),
        in_specs=[
            pl.BlockSpec((t, HEAD_DIM), lambda a, g: (0, a * NSA_GROUPS + g)),
            pl.BlockSpec((1, CMP_BLOCK, HEAD_DIM), lambda a, g: (a, 0, 0)),
            pl.BlockSpec((1, CMP_BLOCK * HEAD_DIM, hid), lambda a, g: (a, 0, 0)),
            pl.BlockSpec((1, hid, HEAD_DIM), lambda a, g: (a, 0, 0)),
        ],
        out_specs=pl.BlockSpec((1, 1, nseg, HEAD_DIM), lambda a, g: (a, g, 0, 0)),
        out_shape=jax.ShapeDtypeStruct((2, NSA_GROUPS, nseg, HEAD_DIM), BF16),
        scratch_shapes=[pltpu.VMEM((nseg, CMP_STRIDE * HEAD_DIM), BF16),
                        pltpu.VMEM((16, CMP_STRIDE * HEAD_DIM), F32)],
        compiler_params=_cparams("arbitrary", "arbitrary"),
    )(kv, pe, w1, w2)


def _group_rows(ref, g):
    return jnp.concatenate([ref[:, (NSA_HPG * g + h) * HEAD_DIM:(NSA_HPG * g + h + 1) * HEAD_DIM]
                            for h in range(NSA_HPG)], axis=0)


def _masked_softmax(s, mask):
    s = jnp.where(mask, s, NEG_BIG)
    m = jnp.max(s, axis=1, keepdims=True)
    e = jnp.where(mask, jnp.exp(s - m), 0.0)
    return e / jnp.maximum(jnp.sum(e, axis=1, keepdims=True), 1e-30)


def _split3(x):
    hi = _bf(x)
    r = x - hi.astype(F32)
    mid = _bf(r)
    return hi, mid, _bf(r - mid.astype(F32))


def _select_blocks(score, blk, n_pick):
    sel = jnp.zeros(score.shape, F32)
    for _ in range(n_pick):
        mx = jnp.max(score, axis=1, keepdims=True)
        idx = jnp.min(jnp.where(score == mx, blk, 1e9), axis=1, keepdims=True)
        pick = blk == idx
        sel = jnp.where(pick, 1.0, sel)
        score = jnp.where(pick, -3.0, score)
    return sel


def _nsa_cmp_kernel(q_ref, cos_ref, sin_ref, kc_ref, vc_ref, oc_ref, sel_ref, qrot_ref, *, n_sel):
    i = pl.program_id(0)
    Q = Q_BLOCK
    nc = kc_ref.shape[1]
    scale = HEAD_DIM ** -0.5
    cosf, sinf = cos_ref[...], sin_ref[...]
    for h in range(NSA_HEADS):
        sl = slice(h * HEAD_DIM, (h + 1) * HEAD_DIM)
        qrot_ref[:, sl] = _bf(_rope(q_ref[:, sl], cosf, sinf))

    qpos_r = i * Q + lax.broadcasted_iota(jnp.int32, (NSA_HPG * Q, nc), 0) % Q
    c_end = lax.broadcasted_iota(jnp.int32, (NSA_HPG * Q, nc), 1) * CMP_STRIDE + (CMP_BLOCK - 1)
    cmask = c_end <= qpos_r
    ci = lax.broadcasted_iota(jnp.int32, (nc, 128), 0) * CMP_STRIDE
    sj = lax.broadcasted_iota(jnp.int32, (nc, 128), 1) * SEL_BLOCK
    cover = jnp.where((ci < sj + SEL_BLOCK) & (ci + CMP_BLOCK > sj), 1.0, 0.0).astype(BF16)
    qpos = i * Q + lax.broadcasted_iota(jnp.int32, (Q, 128), 0)
    blk = lax.broadcasted_iota(jnp.int32, (Q, 128), 1)
    valid = blk * SEL_BLOCK <= qpos
    force = (blk == 0) | (blk >= qpos // SEL_BLOCK - 1)
    blkf = blk.astype(F32)

    for g in range(NSA_GROUPS):
        p = _masked_softmax(_dot_nt(_group_rows(q_ref, g), kc_ref[g]) * scale, cmask)
        oc = _dot(p, vc_ref[g])
        for h in range(NSA_HPG):
            oc_ref[:, (NSA_HPG * g + h) * HEAD_DIM:(NSA_HPG * g + h + 1) * HEAD_DIM] = oc[h * Q:(h + 1) * Q]
        ps = p[0:Q] + p[Q:2 * Q] + p[2 * Q:3 * Q] + p[3 * Q:4 * Q]
        imp = sum(jnp.dot(part, cover, preferred_element_type=F32) for part in _split3(ps))
        score = jnp.where(valid, jnp.where(force, 1e9, imp), -1.0)
        score = jnp.where(blk < n_sel, score, -2.0)
        sel_ref[:, g * 128:(g + 1) * 128] = _bf(_select_blocks(score, blkf, N_SELECT))


def nsa_cmp(q, cosf, sinf, kc, vc):
    t, d = q.shape
    n_sel = t // SEL_BLOCK
    assert N_SELECT <= n_sel <= 128 and t % Q_BLOCK == 0
    nc = kc.shape[1]
    row = lambda w: pl.BlockSpec((Q_BLOCK, w), lambda i: (i, 0))
    full = pl.BlockSpec((NSA_GROUPS, nc, HEAD_DIM), lambda i: (0, 0, 0))
    return pl.pallas_call(
        functools.partial(_nsa_cmp_kernel, n_sel=n_sel),
        name="nsa_cmp",
        grid=(t // Q_BLOCK,),
        in_specs=[row(d), row(HEAD_DIM), row(HEAD_DIM), full, full],
        out_specs=[row(d), row(NSA_GROUPS * 128), row(d)],
        out_shape=[jax.ShapeDtypeStruct((t, d), F32), jax.ShapeDtypeStruct((t, NSA_GROUPS * 128), BF16),
                   jax.ShapeDtypeStruct((t, d), BF16)],
        compiler_params=_cparams("parallel"),
    )(q, cosf, sinf, kc, vc)


SEL_KEYS = 512


def _nsa_sel_kernel(q_ref, sel_ref, k_ref, v_ref, o_ref, m_scr, l_scr, acc_scr):
    i, kt = pl.program_id(0), pl.program_id(1)
    Q = Q_BLOCK
    scale = HEAD_DIM ** -0.5

    @pl.when(kt == 0)
    def _():
        m_scr[...] = jnp.full(m_scr.shape, NEG_BIG, F32)
        l_scr[...] = jnp.zeros(l_scr.shape, F32)
        acc_scr[...] = jnp.zeros(acc_scr.shape, F32)

    @pl.when(kt * SEL_KEYS <= i * Q + Q - 1)
    def _():
        brow = lax.broadcasted_iota(jnp.int32, (128, SEL_KEYS), 0)
        kcol = lax.broadcasted_iota(jnp.int32, (128, SEL_KEYS), 1)
        expand = jnp.where(brow == kt * (SEL_KEYS // SEL_BLOCK) + kcol // SEL_BLOCK, 1.0, 0.0).astype(BF16)
        qpos = i * Q + lax.broadcasted_iota(jnp.int32, (Q, SEL_KEYS), 0)
        kpos = kt * SEL_KEYS + lax.broadcasted_iota(jnp.int32, (Q, SEL_KEYS), 1)
        causal = kpos <= qpos
        for g in range(NSA_GROUPS):
            sl = slice(g * HEAD_DIM, (g + 1) * HEAD_DIM)
            picked = jnp.dot(sel_ref[:, g * 128:(g + 1) * 128], expand, preferred_element_type=F32)
            mask1 = jnp.where((picked > 0.5) & causal, 1.0, 0.0)
            mask = jnp.concatenate([mask1] * NSA_HPG, axis=0) > 0.5
            s = _dot_nt(_group_rows(q_ref, g), k_ref[:, sl]) * scale
            s = jnp.where(mask, s, NEG_BIG)
            m_old = m_scr[g]
            m_new = jnp.maximum(m_old, jnp.max(s, axis=1, keepdims=True))
            alpha = jnp.exp(m_old - m_new)
            e = jnp.where(mask, jnp.exp(s - m_new), 0.0)
            l_scr[g] = alpha * l_scr[g] + jnp.sum(e, axis=1, keepdims=True)
            acc_scr[g] = alpha * acc_scr[g] + _dot(e, v_ref[:, sl])
            m_scr[g] = m_new

    @pl.when(kt == pl.num_programs(1) - 1)
    def _():
        for g in range(NSA_GROUPS):
            o = acc_scr[g] / jnp.maximum(l_scr[g], 1e-30)
            for h in range(NSA_HPG):
                o_ref[:, (NSA_HPG * g + h) * HEAD_DIM:(NSA_HPG * g + h + 1) * HEAD_DIM] = o[h * Q:(h + 1) * Q]


def nsa_sel(qrot, sel, kvb):
    t, d = qrot.shape
    nkt = t // SEL_KEYS
    kblk = lambda sec: pl.BlockSpec(
        (SEL_KEYS, KV_SECTION), lambda i, kt: (jnp.minimum(kt, (i * Q_BLOCK + Q_BLOCK - 1) // SEL_KEYS), sec))
    return pl.pallas_call(
        _nsa_sel_kernel,
        name="nsa_sel",
        grid=(t // Q_BLOCK, nkt),
        in_specs=[pl.BlockSpec((Q_BLOCK, d), lambda i, kt: (i, 0)),
                  pl.BlockSpec((Q_BLOCK, NSA_GROUPS * 128), lambda i, kt: (i, 0)),
                  kblk(2), kblk(3)],
        out_specs=pl.BlockSpec((Q_BLOCK, d), lambda i, kt: (i, 0)),
        out_shape=jax.ShapeDtypeStruct((t, d), F32),
        scratch_shapes=[pltpu.VMEM((NSA_GROUPS, NSA_HPG * Q_BLOCK, 1), F32),
                        pltpu.VMEM((NSA_GROUPS, NSA_HPG * Q_BLOCK, 1), F32),
                        pltpu.VMEM((NSA_GROUPS, NSA_HPG * Q_BLOCK, HEAD_DIM), F32)],
        compiler_params=_cparams("parallel", "arbitrary"),
    )(qrot, sel, kvb, kvb)


WIN_BLOCKS = WINDOW // Q_BLOCK + 1


def _nsa_win_kernel(q_ref, *refs):
    k_refs, v_refs = refs[:WIN_BLOCKS], refs[WIN_BLOCKS:2 * WIN_BLOCKS]
    oc_ref, os_ref, gate_ref, o_ref = refs[2 * WIN_BLOCKS:]
    i = pl.program_id(0)
    Q = Q_BLOCK
    span = WIN_BLOCKS * Q
    scale = HEAD_DIM ** -0.5
    kw = jnp.concatenate([r[...] for r in k_refs], axis=0)
    vw = jnp.concatenate([r[...] for r in v_refs], axis=0)
    qpos = i * Q + lax.broadcasted_iota(jnp.int32, (NSA_HPG * Q, span), 0) % Q
    kpos = (i - (WIN_BLOCKS - 1)) * Q + lax.broadcasted_iota(jnp.int32, (NSA_HPG * Q, span), 1)
    mask = (kpos <= qpos) & (kpos > qpos - WINDOW) & (kpos >= 0)
    gates = _sigmoid(gate_ref[...])
    for g in range(NSA_GROUPS):
        sl = slice(g * HEAD_DIM, (g + 1) * HEAD_DIM)
        p = _masked_softmax(_dot_nt(_group_rows(q_ref, g), kw[:, sl]) * scale, mask)
        ow = _dot(p, vw[:, sl])
        for h in range(NSA_HPG):
            head = NSA_HPG * g + h
            hs = slice(head * HEAD_DIM, (head + 1) * HEAD_DIM)
            o = (gates[:, 3 * head:3 * head + 1] * oc_ref[:, hs] + gates[:, 3 * head + 1:3 * head + 2] * os_ref[:, hs]
                 + gates[:, 3 * head + 2:3 * head + 3] * ow[h * Q:(h + 1) * Q])
            o_ref[:, hs] = _bf(o)


def nsa_win(qrot, kvb, oc, osel, gates):
    t, d = qrot.shape
    row = lambda w: pl.BlockSpec((Q_BLOCK, w), lambda i: (i, 0))
    kblk = lambda sec, b: pl.BlockSpec(
        (Q_BLOCK, KV_SECTION), lambda i: (jnp.maximum(i - (WIN_BLOCKS - 1) + b, 0), sec))
    return pl.pallas_call(
        _nsa_win_kernel,
        name="nsa_win",
        grid=(t // Q_BLOCK,),
        in_specs=([row(d)] + [kblk(4, b) for b in range(WIN_BLOCKS)] + [kblk(5, b) for b in range(WIN_BLOCKS)]
                  + [row(d), row(d), row(128)]),
        out_specs=row(d),
        out_shape=jax.ShapeDtypeStruct((t, d), BF16),
        compiler_params=_cparams("parallel"),
    )(qrot, *([kvb] * (2 * WIN_BLOCKS)), oc, osel, gates)


def nsa_layer_seq(x, nmix, w_q, w_out, kvb, kc, vc, cosf, sinf):
    nq = NSA_HEADS * HEAD_DIM
    q = rms_matmul(x, nmix, w_q, nq, 512)
    gates = rms_matmul(x, nmix, _pad_cols(w_q[:, nq:], 128), 128, 128)
    oc, sel, qrot = nsa_cmp(q, cosf, sinf, kc, vc)
    osel = nsa_sel(qrot, sel, kvb)
    o = nsa_win(qrot, kvb, oc, osel, gates)
    return matmul_res(o, w_out, x)


def _head_column(row, offset, n_heads):
    h = lax.broadcasted_iota(jnp.int32, (n_heads, 128), 0)
    lane = lax.broadcasted_iota(jnp.int32, (n_heads, 128), 1)
    return jnp.sum(jnp.where(lane == h + offset, row, 0.0), axis=1, keepdims=True)


def _pad_rows(rows, n):
    return jnp.concatenate(rows + [jnp.zeros((n - len(rows), rows[0].shape[1]), F32)], axis=0)


def _gdn_step_kernel(proj_ref, ab_ref, past_ref, cw_ref, alog_ref, dtb_ref, nw_ref, s_ref, o_ref, sout_ref):
    H, D = GDN_HEADS, HEAD_DIM
    proj = proj_ref[0]
    cw = cw_ref[...]
    y = cw[GDN_CONV - 1] * proj[0:3 * H]
    for j in range(GDN_CONV - 1):
        y = y + cw[j] * past_ref[0, j]
    y = _silu(y)
    q = _l2norm(y[0:H]) * (D ** -0.5)
    k = _l2norm(y[H:2 * H])
    v = y[2 * H:3 * H]
    ab = ab_ref[0]
    g_row = -jnp.exp(alog_ref[...]) * _softplus(ab + dtb_ref[...])
    eg = jnp.exp(_head_column(g_row, 0, H))
    beta = _head_column(_sigmoid(ab), H, H)
    kb = k * beta
    w = kb * eg
    qe = q * eg
    rs = [_dot(_pad_rows([w[h:h + 1], qe[h:h + 1]], 8), s_ref[0, h]) for h in range(H)]
    ws = jnp.concatenate([r[0:1] for r in rs], axis=0)
    qs = jnp.concatenate([r[1:2] for r in rs], axis=0)
    v_new = v * beta - ws
    o = qs + jnp.sum(q * k, axis=1, keepdims=True) * v_new
    for h in range(H):
        outer = _dot_tn(_pad_rows([k[h:h + 1]], 8), _pad_rows([v_new[h:h + 1]], 8))
        sout_ref[0, h] = s_ref[0, h] * eg[h:h + 1] + outer
    o_ref[0] = _bf(_rms(o, nw_ref[...]) * _silu(proj[3 * H:4 * H]))


def gdn_step(proj, ab, conv_past, conv_w, a_log, dt_bias, norm_w, s0):
    b = proj.shape[0]
    H, D = GDN_HEADS, HEAD_DIM
    pad16 = lambda a: jnp.pad(a.reshape(1, H), ((0, 0), (0, 128 - H)))
    o, s_out = pl.pallas_call(
        _gdn_step_kernel,
        name="gdn_step",
        grid=(b,),
        in_specs=[pl.BlockSpec((1, 4 * H, D), lambda i: (i, 0, 0)),
                  pl.BlockSpec((1, 1, 128), lambda i: (i, 0, 0)),
                  pl.BlockSpec((1, GDN_CONV - 1, 3 * H, D), lambda i: (i, 0, 0, 0)),
                  pl.BlockSpec((GDN_CONV, 3 * H, D), lambda i: (0, 0, 0)),
                  pl.BlockSpec((1, 128), lambda i: (0, 0)),
                  pl.BlockSpec((1, 128), lambda i: (0, 0)),
                  pl.BlockSpec((1, D), lambda i: (0, 0)),
                  pl.BlockSpec((1, H, D, D), lambda i: (i, 0, 0, 0))],
        out_specs=[pl.BlockSpec((1, H, D), lambda i: (i, 0, 0)),
                   pl.BlockSpec((1, H, D, D), lambda i: (i, 0, 0, 0))],
        out_shape=[jax.ShapeDtypeStruct((b, H, D), BF16), jax.ShapeDtypeStruct((b, H, D, D), F32)],
        compiler_params=_cparams("parallel"),
    )(proj.reshape(b, 4 * H, D), ab.reshape(b, 1, 128), conv_past.reshape(b, GDN_CONV - 1, 3 * H, D),
      conv_w.reshape(GDN_CONV, 3 * H, D), pad16(a_log), pad16(dt_bias), norm_w.reshape(1, D), s0)
    return o.reshape(b, H * D), s_out


def gdn_layer_step(x, nmix, w_in, conv_w, a_log, dt_bias, norm_w, w_out, conv_past, s0):
    nqkv = 3 * GDN_HEADS * HEAD_DIM
    nz = GDN_HEADS * HEAD_DIM
    proj = rms_matmul(x, nmix, w_in, nqkv + nz, 512)
    ab = rms_matmul(x, nmix, _pad_cols(w_in[:, nqkv + nz:], 128), 128, 128)
    o, s_out = gdn_step(proj, ab, conv_past, conv_w, a_log, dt_bias, norm_w, s0)
    conv_new = jnp.concatenate([conv_past[:, 1:], proj[:, None, :nqkv]], axis=1)
    return matmul_res(o, w_out, x), s_out, conv_new


def _compress_pages_kernel(pt_ref, *refs, n_pages):
    page_refs = refs[:n_pages]
    pe_ref, w1_ref, w2_ref, o_ref, xs_ref, ps_ref = refs[n_pages:]
    G, D = NSA_GROUPS, HEAD_DIM
    seg_pp = page_refs[0].shape[1]
    nseg = n_pages * seg_pp
    half = CMP_STRIDE * D
    ps_ref[...] = jnp.zeros(ps_ref.shape, F32)
    for a in range(2):
        for t in range(CMP_STRIDE):
            sl = slice(t * D, (t + 1) * D)
            ps_ref[a, 0:1, sl] = pe_ref[a, t:t + 1, :]
            ps_ref[a, 8:9, sl] = pe_ref[a, CMP_STRIDE + t:CMP_STRIDE + t + 1, :]
    for p in range(n_pages):
        for t in range(CMP_STRIDE):
            rows = page_refs[p][0, :, t, :]
            for a in range(2):
                for g in range(G):
                    xs_ref[a, g * nseg + p * seg_pp:g * nseg + (p + 1) * seg_pp, t * D:(t + 1) * D] = (
                        rows[:, (a * G + g) * D:(a * G + g + 1) * D])
    for a in range(2):
        xs = _bf(xs_ref[a])
        w_lo = w1_ref[a, 0:half, :]
        w_hi = w1_ref[a, half:2 * half, :]
        lo = jnp.dot(xs, w_lo, preferred_element_type=F32) + _dot(ps_ref[a, 0:8, :], w_lo)[0:1]
        hi = jnp.dot(xs, w_hi, preferred_element_type=F32) + _dot(ps_ref[a, 8:16, :], w_hi)[0:1]
        hi = jnp.concatenate([pltpu.roll(hi[g * nseg:(g + 1) * nseg], nseg - 1, 0) for g in range(G)], axis=0)
        o_ref[0, a] = _bf(_dot(_silu(lo + hi), w2_ref[a]))


def compress_pages(cache, page_table, pe, w1b, w2):
    b, n_pages = page_table.shape
    page, width = cache.shape[1], cache.shape[2]
    seg_pp = page // CMP_STRIDE
    nseg = n_pages * seg_pp
    hid = w1b.shape[2]
    cache = cache.reshape(cache.shape[0], seg_pp, CMP_STRIDE, width)

    def page_spec(p):
        return pl.BlockSpec((1, seg_pp, CMP_STRIDE, width), lambda i, pt: (pt[i, p], 0, 0, 0))

    const = lambda shape: pl.BlockSpec(shape, lambda i, pt: (0,) * len(shape))
    return pl.pallas_call(
        functools.partial(_compress_pages_kernel, n_pages=n_pages),
        name="compress_pages",
        grid_spec=pltpu.PrefetchScalarGridSpec(
            num_scalar_prefetch=1,
            grid=(b,),
            in_specs=[page_spec(p) for p in range(n_pages)] + [
                const((2, CMP_BLOCK, HEAD_DIM)), const((2, CMP_BLOCK * HEAD_DIM, hid)), const((2, hid, HEAD_DIM))],
            out_specs=pl.BlockSpec((1, 2, NSA_GROUPS * nseg, HEAD_DIM), lambda i, pt: (i, 0, 0, 0)),
            scratch_shapes=[pltpu.VMEM((2, NSA_GROUPS * nseg, CMP_STRIDE * HEAD_DIM), F32),
                            pltpu.VMEM((2, 16, CMP_STRIDE * HEAD_DIM), F32)],
        ),
        out_shape=jax.ShapeDtypeStruct((b, 2, NSA_GROUPS * nseg, HEAD_DIM), BF16),
        compiler_params=_cparams("parallel"),
    )(page_table, *([cache] * n_pages), pe, w1b, w2)


def _head_rows_of_group(x, g):
    h = lax.broadcasted_iota(jnp.int32, x.shape, 0)
    return jnp.where(h // NSA_HPG == g, x, 0.0)


def _nsa_step_kernel(pt_ref, *refs, n_pages):
    page_refs = refs[:n_pages]
    q_ref, gate_ref, cos_ref, sin_ref, kvc_ref, new_ref, win_ref, o_ref, nwin_ref = refs[n_pages:]
    H, G, D = NSA_HEADS, NSA_GROUPS, HEAD_DIM
    page = page_refs[0].shape[1]
    past = n_pages * page
    nseg = past // CMP_STRIDE
    n_sel = past // SEL_BLOCK + 1
    wb = win_ref.shape[1]
    scale = D ** -0.5
    q_raw = q_ref[0]
    q_rot = _rope(q_raw, cos_ref[...], sin_ref[...])
    new = new_ref[0]

    nwin_ref[0, 0:wb - 1, :] = win_ref[0, 1:wb, :]
    nwin_ref[0, wb - 1:wb, :] = new[:, 4 * KV_SECTION:6 * KV_SECTION]

    lane = lax.broadcasted_iota(jnp.int32, (8, 128), 1)
    blkf = lane.astype(F32)
    c_ok = lax.broadcasted_iota(jnp.int32, (H, nseg), 1) * CMP_STRIDE + (CMP_BLOCK - 1) <= past
    ci = lax.broadcasted_iota(jnp.int32, (nseg, 128), 0) * CMP_STRIDE
    sj = lax.broadcasted_iota(jnp.int32, (nseg, 128), 1) * SEL_BLOCK
    cover = jnp.where((ci < sj + SEL_BLOCK) & (ci + CMP_BLOCK > sj), 1.0, 0.0).astype(BF16)
    valid = lane * SEL_BLOCK <= past
    force = (lane == 0) | (lane >= past // SEL_BLOCK - 1)
    brow = lax.broadcasted_iota(jnp.int32, (128, past), 0)
    kcol = lax.broadcasted_iota(jnp.int32, (128, past), 1)
    expand = jnp.where(brow == kcol // SEL_BLOCK, 1.0, 0.0).astype(BF16)
    all_keys = jnp.full((H, wb), True)

    o_cmp = jnp.zeros((H, D), F32)
    o_sel = jnp.zeros((H, D), F32)
    o_win = jnp.zeros((H, D), F32)
    for g in range(G):
        sl = slice(g * D, (g + 1) * D)
        p = _masked_softmax(_dot_nt(q_raw, kvc_ref[0, 0, g * nseg:(g + 1) * nseg, :]) * scale, c_ok)
        o_cmp = o_cmp + _head_rows_of_group(_dot(p, kvc_ref[0, 1, g * nseg:(g + 1) * nseg, :]), g)
        ps = jnp.sum(_head_rows_of_group(p, g), axis=0, keepdims=True)
        ps = jnp.broadcast_to(ps, (8, nseg))
        imp = sum(jnp.dot(part, cover, preferred_element_type=F32) for part in _split3(ps))
        score = jnp.where(valid, jnp.where(force, 1e9, imp), -1.0)
        score = jnp.where(lane < n_sel, score, -2.0)
        sel = _select_blocks(score, blkf, N_SELECT)
        picked = jnp.dot(_bf(sel), expand, preferred_element_type=F32)
        kmask = jnp.concatenate([picked, picked], axis=0) > 0.5
        kg = jnp.concatenate([_bf(r[0, :, sl]) for r in page_refs], axis=0)
        vg = jnp.concatenate([_bf(r[0, :, G * D + g * D:G * D + (g + 1) * D]) for r in page_refs], axis=0)
        s = jnp.where(kmask, _dot_nt(q_rot, kg) * scale, NEG_BIG)
        k_new = new[:, 2 * KV_SECTION + g * D:2 * KV_SECTION + (g + 1) * D]
        v_new = new[:, 3 * KV_SECTION + g * D:3 * KV_SECTION + (g + 1) * D]
        new_ok = jnp.sum(jnp.where(lane == n_sel - 1, sel, 0.0), axis=1, keepdims=True)[0:1] > 0.5
        s_new = jnp.where(new_ok, jnp.sum(q_rot * k_new, axis=1, keepdims=True) * scale, NEG_BIG)
        m = jnp.maximum(jnp.max(s, axis=1, keepdims=True), s_new)
        e = jnp.where(kmask, jnp.exp(s - m), 0.0)
        e_new = jnp.where(new_ok, jnp.exp(s_new - m), 0.0)
        den = jnp.maximum(jnp.sum(e, axis=1, keepdims=True) + e_new, 1e-30)
        o_sel = o_sel + _head_rows_of_group((_dot(e, vg) + e_new * v_new) / den, g)
        pw = _masked_softmax(_dot_nt(q_rot, nwin_ref[0, :, sl]) * scale, all_keys)
        o_win = o_win + _head_rows_of_group(_dot(pw, nwin_ref[0, :, G * D + g * D:G * D + (g + 1) * D]), g)

    gates = _sigmoid(gate_ref[0])
    hh = lax.broadcasted_iota(jnp.int32, (H, 128), 0)
    ll = lax.broadcasted_iota(jnp.int32, (H, 128), 1)
    gcol = lambda c: jnp.sum(jnp.where(ll == 3 * hh + c, gates, 0.0), axis=1, keepdims=True)
    o_ref[0] = _bf(gcol(0) * o_cmp + gcol(1) * o_sel + gcol(2) * o_win)


def nsa_step(q, gates, cosf, sinf, kvc, kv_new, cache_sel, cache_win, page_table):
    b, n_pages = page_table.shape
    page, width = cache_sel.shape[1], cache_sel.shape[2]
    wb = cache_win.shape[1]
    H, D = NSA_HEADS, HEAD_DIM

    def page_spec(p):
        return pl.BlockSpec((1, page, width), lambda i, pt: (pt[i, p], 0, 0))

    per_seq = lambda *shape: pl.BlockSpec((1,) + shape, lambda i, pt: (i,) + (0,) * len(shape))
    const = lambda *shape: pl.BlockSpec(shape, lambda i, pt: (0,) * len(shape))
    o, nwin = pl.pallas_call(
        functools.partial(_nsa_step_kernel, n_pages=n_pages),
        name="nsa_step",
        grid_spec=pltpu.PrefetchScalarGridSpec(
            num_scalar_prefetch=1,
            grid=(b,),
            in_specs=[page_spec(p) for p in range(n_pages)] + [
                per_seq(H, D), per_seq(1, 128), const(1, D), const(1, D),
                per_seq(2, kvc.shape[2], D), per_seq(1, kv_new.shape[1]), per_seq(wb, width)],
            out_specs=[per_seq(H, D), per_seq(wb, width)],
        ),
        out_shape=[jax.ShapeDtypeStruct((b, H, D), BF16), jax.ShapeDtypeStruct((b, wb, width), F32)],
        compiler_params=_cparams("parallel"),
    )(page_table, *([cache_sel] * n_pages), q.reshape(b, H, D), gates.reshape(b, 1, 128), cosf, sinf,
      kvc, kv_new.reshape(b, 1, kv_new.shape[1]), cache_win)
    return o.reshape(b, H * D), nwin


def kernel(x_prompt, x_sample, state_gdn, state_gdn_conv, state_ffn_conv, cache_cmp_kv, cache_sel_kv, cache_win_kv, page_table, norm_mixer, norm_ffn, norm_kv, norm_final, gdn_w_in, gdn_conv_w, gdn_A_log, gdn_dt_bias, gdn_norm_w, gdn_w_out, nsa_w_q, nsa_w_out, kv_w, cmp_pe_k, cmp_pe_v, cmp_w1_k, cmp_w2_k, cmp_w1_v, cmp_w2_v, ffn_w_gate, ffn_w_up, ffn_conv_w, ffn_conv_b, ffn_w_down):
    cmp_pe = jnp.stack([cmp_pe_k, cmp_pe_v])
    cmp_w1 = jnp.stack([cmp_w1_k, cmp_w1_v])
    cmp_w2 = jnp.stack([cmp_w2_k, cmp_w2_v])

    x = x_prompt[0]
    t = x.shape[0]
    x, gdn_s_p, gdn_c_p = gdn_layer_seq(x, norm_mixer[0], gdn_w_in[0], gdn_conv_w[0], gdn_A_log[0], gdn_dt_bias[0],
                                        gdn_norm_w[0], gdn_w_out[0])
    x, ffn_c0_p = ffn_layer_seq(x, norm_ffn[0], ffn_w_gate[0], ffn_w_up[0], ffn_conv_w[0], ffn_conv_b[0],
                                ffn_w_down[0])
    cosf, sinf = _rope_tables(jnp.arange(t, dtype=jnp.int32))
    kv, kvb = kv_rows(x, norm_kv, kv_w, cosf, sinf)
    kvc = compress_seq(kv, cmp_pe, cmp_w1, cmp_w2)
    x = nsa_layer_seq(x, norm_mixer[1], nsa_w_q[0], nsa_w_out[0], kvb, kvc[0], kvc[1], cosf, sinf)
    y_p, ffn_c1_p = ffn_layer_seq(x, norm_ffn[1], ffn_w_gate[1], ffn_w_up[1], ffn_conv_w[1], ffn_conv_b[1],
                                  ffn_w_down[1], final_nw=norm_final)
    g, hd = NSA_GROUPS, HEAD_DIM
    nrow = 2 * g * hd
    cmp_p = kv[:, :nrow].reshape(1, t, 2, g, hd)
    sel_p = kv[:, nrow:2 * nrow].reshape(1, t, 2, g, hd)
    win_p = kv[t - min(WINDOW, t):, 2 * nrow:].reshape(1, min(WINDOW, t), 2, g, hd)
    assert x_sample.shape[1] == 1
    xs = x_sample[:, 0]
    b = xs.shape[0]
    n_pool, page = cache_sel_kv.shape[:2]
    past_len = page_table.shape[1] * page
    wb = cache_win_kv.shape[1]
    xs, gdn_s_s, gdn_c_s = gdn_layer_step(xs, norm_mixer[0], gdn_w_in[0], gdn_conv_w[0], gdn_A_log[0], gdn_dt_bias[0],
                                          gdn_norm_w[0], gdn_w_out[0], state_gdn_conv[0], state_gdn[0])
    xs, ffn_c0_s = ffn_layer_step(xs, norm_ffn[0], ffn_w_gate[0], ffn_w_up[0], ffn_conv_w[0], ffn_conv_b[0],
                                  ffn_w_down[0], state_ffn_conv[0])
    cos1, sin1 = _rope_tables(jnp.full((1,), past_len, jnp.int32))
    kv_s, _ = kv_rows(xs, norm_kv, kv_w, jnp.broadcast_to(cos1, (b, hd)), jnp.broadcast_to(sin1, (b, hd)))
    kvc_s = compress_pages(cache_cmp_kv.reshape(n_pool, page, nrow), page_table, cmp_pe, _bf(cmp_w1), cmp_w2)
    nq = NSA_HEADS * hd
    q_s = rms_matmul(xs, norm_mixer[1], nsa_w_q[0], nq, 512)
    gates_s = rms_matmul(xs, norm_mixer[1], _pad_cols(nsa_w_q[0][:, nq:], 128), 128, 128)
    o_s, nwin = nsa_step(q_s, gates_s, cos1, sin1, kvc_s, kv_s, cache_sel_kv.reshape(n_pool, page, nrow),
                         cache_win_kv.reshape(b, wb, nrow), page_table)
    xs = matmul_res(o_s, nsa_w_out[0], xs)
    y_s, ffn_c1_s = ffn_layer_step(xs, norm_ffn[1], ffn_w_gate[1], ffn_w_up[1], ffn_conv_w[1], ffn_conv_b[1],
                                   ffn_w_down[1], state_ffn_conv[1], final_nw=norm_final)
    cmp_s = kv_s[:, :nrow].reshape(b, 1, 2, g, hd)
    sel_s = kv_s[:, nrow:2 * nrow].reshape(b, 1, 2, g, hd)
    win_s = nwin.reshape(b, wb, 2, g, hd)

    return (y_p[None], y_s[:, None],
            gdn_s_p[None, None], gdn_c_p[None, None], jnp.stack([ffn_c0_p, ffn_c1_p])[:, None], cmp_p, sel_p, win_p,
            gdn_s_s[None], gdn_c_s[None], jnp.stack([ffn_c0_s, ffn_c1_s]), cmp_s, sel_s, win_s)
```

```python
import functools

import jax
import jax.numpy as jnp
import numpy as np
from jax import lax
from jax.experimental import pallas as pl
from jax.experimental.pallas import tpu as pltpu

F32 = jnp.float32
BF16 = jnp.bfloat16

RMS_EPS = 1e-6
ROPE_THETA = 10000.0
HEAD_DIM = 128
GDN_HEADS = 16
GDN_CONV = 4
GDN_CHUNK = 128
NSA_HEADS = 16
NSA_GROUPS = 4
NSA_HPG = NSA_HEADS // NSA_GROUPS
CMP_STRIDE = 16
CMP_BLOCK = 32
SEL_BLOCK = 64
N_SELECT = 16
WINDOW = 512
Q_BLOCK = 128
FFN_CONV = 3
NEG_BIG = -1e30
LOG2E = 1.4426950408889634

VMEM_LIMIT = 52 * 1024 * 1024


def _cparams(*sem):
    return pltpu.CompilerParams(dimension_semantics=sem, vmem_limit_bytes=VMEM_LIMIT)


def _bf(x):
    return x.astype(BF16)


def _dot(a, b):
    return jnp.dot(_bf(a), _bf(b), preferred_element_type=F32)


def _dot_nt(a, b):
    return lax.dot_general(_bf(a), _bf(b), (((1,), (1,)), ((), ())), preferred_element_type=F32)


def _dot_tn(a, b):
    return lax.dot_general(_bf(a), _bf(b), (((0,), (0,)), ((), ())), preferred_element_type=F32)


def _sigmoid(x):
    return 1.0 / (1.0 + jnp.exp(-x))


def _silu(x):
    return x * _sigmoid(x)


def _rms(x, w):
    return x * lax.rsqrt(jnp.mean(x * x, axis=-1, keepdims=True) + RMS_EPS) * w


def _rms_mm_kernel(x_ref, nw_ref, w_ref, o_ref, h_ref):
    @pl.when(pl.program_id(1) == 0)
    def _():
        h_ref[...] = _bf(_rms(x_ref[...], nw_ref[...]))

    o_ref[...] = jnp.dot(h_ref[...], _bf(w_ref[...]), preferred_element_type=F32).astype(o_ref.dtype)


def rms_matmul(x, nw, w, n_out, tn, tm=1024, out_dtype=F32):
    m, k = x.shape
    tm = min(tm, m)
    return pl.pallas_call(
        _rms_mm_kernel,
        name="rms_mm",
        grid=(m // tm, n_out // tn),
        in_specs=[
            pl.BlockSpec((tm, k), lambda i, j: (i, 0)),
            pl.BlockSpec((1, k), lambda i, j: (0, 0)),
            pl.BlockSpec((k, tn), lambda i, j: (0, j)),
        ],
        out_specs=pl.BlockSpec((tm, tn), lambda i, j: (i, j)),
        out_shape=jax.ShapeDtypeStruct((m, n_out), out_dtype),
        scratch_shapes=[pltpu.VMEM((tm, k), BF16)],
        compiler_params=_cparams("parallel", "arbitrary"),
    )(x, nw.reshape(1, k), w)


def _mm_res_kernel(a_ref, w_ref, r_ref, *rest, final_norm):
    if final_norm:
        nw_ref, o_ref, acc_ref = rest
    else:
        o_ref, acc_ref = rest
    kk = pl.program_id(1)

    @pl.when(kk == 0)
    def _():
        acc_ref[...] = r_ref[...]

    acc_ref[...] += jnp.dot(a_ref[...], _bf(w_ref[...]), preferred_element_type=F32)

    @pl.when(kk == pl.num_programs(1) - 1)
    def _():
        if final_norm:
            o_ref[...] = _rms(acc_ref[...], nw_ref[...])
        else:
            o_ref[...] = acc_ref[...]


def matmul_res(a, w, res, final_nw=None, tm=512, tk=512):
    m, k = a.shape
    n = w.shape[1]
    tm = min(tm, m)
    in_specs = [
        pl.BlockSpec((tm, tk), lambda i, kk: (i, kk)),
        pl.BlockSpec((tk, n), lambda i, kk: (kk, 0)),
        pl.BlockSpec((tm, n), lambda i, kk: (i, 0)),
    ]
    args = [a, w, res]
    if final_nw is not None:
        in_specs.append(pl.BlockSpec((1, n), lambda i, kk: (0, 0)))
        args.append(final_nw.reshape(1, n))
    return pl.pallas_call(
        functools.partial(_mm_res_kernel, final_norm=final_nw is not None),
        name="mm_res",
        grid=(m // tm, k // tk),
        in_specs=in_specs,
        out_specs=pl.BlockSpec((tm, n), lambda i, kk: (i, 0)),
        out_shape=jax.ShapeDtypeStruct((m, n), F32),
        scratch_shapes=[pltpu.VMEM((tm, n), F32)],
        compiler_params=_cparams("parallel", "arbitrary"),
    )(*args)


def _ffn_up_seq_kernel(x_ref, nw_ref, wg_ref, wu_ref, cw_ref, cb_ref, act_ref, st_ref, h_ref, carry_ref, buf_ref):
    i, j = pl.program_id(0), pl.program_id(1)
    tm = x_ref.shape[0]

    @pl.when(j == 0)
    def _():
        h_ref[...] = _bf(_rms(x_ref[...], nw_ref[...]))

    @pl.when(i == 0)
    def _():
        carry_ref[j] = jnp.zeros(carry_ref.shape[1:], F32)

    h = h_ref[...]
    g = jnp.dot(h, _bf(wg_ref[...]), preferred_element_type=F32)
    u = jnp.dot(h, _bf(wu_ref[...]), preferred_element_type=F32)
    buf_ref[0:8, :] = carry_ref[j]
    buf_ref[8:8 + tm, :] = g
    cw = cw_ref[...]
    a = cw[0:1] * buf_ref[6:6 + tm, :] + cw[1:2] * buf_ref[7:7 + tm, :] + cw[2:3] * g
    act_ref[...] = _bf(_silu(a + cb_ref[...]) * u)
    carry_ref[j] = g[tm - 8:tm]
    st_ref[0] = g[tm - 8:tm]


def ffn_up_seq(x, nw, wg, wu, cw, cb, tm=1024, tn=512):
    m, k = x.shape
    f = wg.shape[1]
    nb = f // tn
    act, st = pl.pallas_call(
        _ffn_up_seq_kernel,
        name="ffn_up_seq",
        grid=(m // tm, nb),
        in_specs=[
            pl.BlockSpec((tm, k), lambda i, j: (i, 0)),
            pl.BlockSpec((1, k), lambda i, j: (0, 0)),
            pl.BlockSpec((k, tn), lambda i, j: (0, j)),
            pl.BlockSpec((k, tn), lambda i, j: (0, j)),
            pl.BlockSpec((FFN_CONV, tn), lambda i, j: (0, j)),
            pl.BlockSpec((1, tn), lambda i, j: (0, j)),
        ],
        out_specs=[
            pl.BlockSpec((tm, tn), lambda i, j: (i, j)),
            pl.BlockSpec((1, 8, tn), lambda i, j: (i, 0, j)),
        ],
        out_shape=[
            jax.ShapeDtypeStruct((m, f), BF16),
            jax.ShapeDtypeStruct((m // tm, 8, f), F32),
        ],
        scratch_shapes=[
            pltpu.VMEM((tm, k), BF16),
            pltpu.VMEM((nb, 8, tn), F32),
            pltpu.VMEM((tm + 8, tn), F32),
        ],
        compiler_params=_cparams("arbitrary", "arbitrary"),
    )(x, nw.reshape(1, k), wg, wu, cw, cb.reshape(1, f))
    return act, st[-1, 8 - (FFN_CONV - 1):]


def _ffn_up_step_kernel(x_ref, nw_ref, wg_ref, wu_ref, cw_ref, cb_ref, p0_ref, p1_ref, act_ref, g_ref, h_ref):
    @pl.when(pl.program_id(1) == 0)
    def _():
        h_ref[...] = _bf(_rms(x_ref[...], nw_ref[...]))

    h = h_ref[...]
    g = jnp.dot(h, _bf(wg_ref[...]), preferred_element_type=F32)
    u = jnp.dot(h, _bf(wu_ref[...]), preferred_element_type=F32)
    cw = cw_ref[...]
    a = cw[0:1] * p0_ref[...] + cw[1:2] * p1_ref[...] + cw[2:3] * g
    act_ref[...] = _bf(_silu(a + cb_ref[...]) * u)
    g_ref[...] = g


def ffn_up_step(x, nw, wg, wu, cw, cb, past, tn=512):
    m, k = x.shape
    f = wg.shape[1]
    nb = f // tn
    past2 = past.reshape(m, (FFN_CONV - 1) * f)
    act, g = pl.pallas_call(
        _ffn_up_step_kernel,
        name="ffn_up_step",
        grid=(1, nb),
        in_specs=[
            pl.BlockSpec((m, k), lambda i, j: (0, 0)),
            pl.BlockSpec((1, k), lambda i, j: (0, 0)),
            pl.BlockSpec((k, tn), lambda i, j: (0, j)),
            pl.BlockSpec((k, tn), lambda i, j: (0, j)),
            pl.BlockSpec((FFN_CONV, tn), lambda i, j: (0, j)),
            pl.BlockSpec((1, tn), lambda i, j: (0, j)),
            pl.BlockSpec((m, tn), lambda i, j: (0, j)),
            pl.BlockSpec((m, tn), lambda i, j: (0, j + nb)),
        ],
        out_specs=[
            pl.BlockSpec((m, tn), lambda i, j: (0, j)),
            pl.BlockSpec((m, tn), lambda i, j: (0, j)),
        ],
        out_shape=[
            jax.ShapeDtypeStruct((m, f), BF16),
            jax.ShapeDtypeStruct((m, f), F32),
        ],
        scratch_shapes=[pltpu.VMEM((m, k), BF16)],
        compiler_params=_cparams("arbitrary", "arbitrary"),
    )(x, nw.reshape(1, k), wg, wu, cw, cb.reshape(1, f), past2, past2)
    return act, jnp.stack([past[:, 1], g], axis=1)


GDN_HPS = 8


def _cumsum_rows(x):
    row = lax.broadcasted_iota(jnp.int32, x.shape, 0)
    s = 1
    while s < x.shape[0]:
        x = x + jnp.where(row >= s, pltpu.roll(x, s, 0), 0.0)
        s *= 2
    return x


def _softplus(x):
    return jnp.maximum(x, 0.0) + jnp.log1p(jnp.exp(-jnp.abs(x)))


def _l2norm(x):
    return x * lax.rsqrt(jnp.sum(x * x, axis=-1, keepdims=True) + 1e-6)


def _gdn_seq_kernel(qp_ref, kp_ref, vp_ref, qh_ref, kh_ref, vh_ref, qpast_ref, kpast_ref, vpast_ref,
                    qw_ref, kw_ref, vw_ref, z_ref, ab_ref, alog_ref, dtb_ref, nw_ref, s0_ref,
                    o_ref, sout_ref, s_scr, buf_ref):
    hg, c = pl.program_id(0), pl.program_id(1)
    C = GDN_CHUNK
    D = HEAD_DIM

    @pl.when(c == 0)
    def _():
        s_scr[...] = s0_ref[...]

    def conv(p_ref, h_ref, past_ref, w_ref):
        buf_ref[0:8, :] = jnp.where(c == 0, past_ref[...], h_ref[...])
        buf_ref[8:8 + C, :] = p_ref[...]
        w = w_ref[...]
        y = (w[0:1] * buf_ref[5:5 + C, :] + w[1:2] * buf_ref[6:6 + C, :]
             + w[2:3] * buf_ref[7:7 + C, :] + w[3:4] * buf_ref[8:8 + C, :])
        return _silu(y)

    qc = conv(qp_ref, qh_ref, qpast_ref, qw_ref)
    kc = conv(kp_ref, kh_ref, kpast_ref, kw_ref)
    vc = conv(vp_ref, vh_ref, vpast_ref, vw_ref)

    ab = ab_ref[...]
    lane = lax.broadcasted_iota(jnp.int32, (C, 128), 1)
    g_all = -jnp.exp(alog_ref[...]) * _softplus(ab + dtb_ref[...])
    gc_all = _cumsum_rows(g_all)
    beta_all = _sigmoid(ab)

    row = lax.broadcasted_iota(jnp.int32, (C, C), 0)
    col = lax.broadcasted_iota(jnp.int32, (C, C), 1)
    eye = (row == col).astype(F32)
    nw = nw_ref[...]

    hs = range(GDN_HPS)
    sl = [slice(j * D, (j + 1) * D) for j in hs]
    gcol = [jnp.sum(jnp.where(lane == hg * GDN_HPS + j, gc_all, 0.0), axis=1, keepdims=True) for j in hs]
    bcol = [jnp.sum(jnp.where(lane == hg * GDN_HPS + j + GDN_HEADS, beta_all, 0.0), axis=1, keepdims=True)
            for j in hs]
    colb = [jnp.broadcast_to(gcol[j], (C, C)) for j in hs]
    dec = [jnp.exp(jnp.minimum(colb[j] - colb[j].T, 0.0)) for j in hs]
    q = [_l2norm(qc[:, sl[j]]) * (D ** -0.5) for j in hs]
    k = [_l2norm(kc[:, sl[j]]) for j in hs]
    kb = [k[j] * bcol[j] for j in hs]
    a_dec = [_dot_nt(kb[j], k[j]) * dec[j] for j in hs]
    qk = [jnp.where(row >= col, _dot_nt(q[j], k[j]) * dec[j], 0.0) for j in hs]
    first = ((row ^ col) == 1) & ((row & 1) == 1)
    tm = [eye - jnp.where(first, a_dec[j], 0.0) for j in hs]
    s = 2
    while s < C:
        msk = ((row // (2 * s)) == (col // (2 * s))) & ((row & s) != 0) & ((col & s) == 0)
        x = [_dot(jnp.where(msk, a_dec[j], 0.0), tm[j]) for j in hs]
        tm = [tm[j] - _dot(tm[j], x[j]) for j in hs]
        s *= 2
    egc = [jnp.exp(gcol[j]) for j in hs]
    uw = [_dot(tm[j], jnp.concatenate([vc[:, sl[j]] * bcol[j], kb[j] * egc[j]], axis=1)) for j in hs]
    r = [_dot(jnp.concatenate([uw[j][:, D:], q[j] * egc[j]], axis=0), s_scr[j]) for j in hs]
    v_new = [uw[j][:, :D] - r[j][:C] for j in hs]
    o = [r[j][C:] + _dot(qk[j], v_new[j]) for j in hs]
    for j in hs:
        g_last = colb[j][C - 1:C, :]
        kdec = k[j] * jnp.exp(g_last[:, 0:1] - gcol[j])
        s_scr[j] = s_scr[j] * jnp.exp(g_last) + _dot_tn(kdec, v_new[j])
        o_ref[:, sl[j]] = _bf(_rms(o[j], nw) * _silu(z_ref[:, sl[j]]))

    @pl.when(c == pl.num_programs(1) - 1)
    def _():
        sout_ref[...] = s_scr[...]


def gdn_seq(proj, ab, conv_w, a_log, dt_bias, norm_w, past8, s0):
    t = proj.shape[0]
    C, W = GDN_CHUNK, GDN_HPS * HEAD_DIM
    nq = GDN_HEADS * HEAD_DIM // W
    pad16 = lambda a: jnp.pad(a.reshape(1, GDN_HEADS), ((0, 0), (0, 128 - GDN_HEADS)))

    def rows(off):
        return pl.BlockSpec((C, W), lambda hg, c: (c, off + hg))

    def halo(off):
        return pl.BlockSpec((8, W), lambda hg, c: (jnp.maximum(c * (C // 8) - 1, 0), off + hg))

    def fixed(rws, off):
        return pl.BlockSpec((rws, W), lambda hg, c: (0, off + hg))

    o, s_out = pl.pallas_call(
        _gdn_seq_kernel,
        name="gdn_seq",
        grid=(nq, t // C),
        in_specs=[rows(0), rows(nq), rows(2 * nq), halo(0), halo(nq), halo(2 * nq),
                  fixed(8, 0), fixed(8, nq), fixed(8, 2 * nq),
                  fixed(GDN_CONV, 0), fixed(GDN_CONV, nq), fixed(GDN_CONV, 2 * nq),
                  rows(3 * nq),
                  pl.BlockSpec((C, 128), lambda hg, c: (c, 0)),
                  pl.BlockSpec((1, 128), lambda hg, c: (0, 0)),
                  pl.BlockSpec((1, 128), lambda hg, c: (0, 0)),
                  pl.BlockSpec((1, HEAD_DIM), lambda hg, c: (0, 0)),
                  pl.BlockSpec((GDN_HPS, HEAD_DIM, HEAD_DIM), lambda hg, c: (hg, 0, 0))],
        out_specs=[pl.BlockSpec((C, W), lambda hg, c: (c, hg)),
                   pl.BlockSpec((GDN_HPS, HEAD_DIM, HEAD_DIM), lambda hg, c: (hg, 0, 0))],
        out_shape=[jax.ShapeDtypeStruct((t, GDN_HEADS * HEAD_DIM), BF16),
                   jax.ShapeDtypeStruct((GDN_HEADS, HEAD_DIM, HEAD_DIM), F32)],
        scratch_shapes=[pltpu.VMEM((GDN_HPS, HEAD_DIM, HEAD_DIM), F32),
                        pltpu.VMEM((C + 8, W), F32)],
        compiler_params=_cparams("parallel", "arbitrary"),
    )(proj, proj, proj, proj, proj, proj, past8, past8, past8,
      conv_w, conv_w, conv_w, proj, ab, pad16(a_log), pad16(dt_bias), norm_w.reshape(1, HEAD_DIM), s0)
    return o, s_out


def _pad_cols(w, n):
    return jnp.pad(w, ((0, 0), (0, n - w.shape[1])))


def gdn_layer_seq(x, nmix, w_in, conv_w, a_log, dt_bias, norm_w, w_out):
    nqkv = 3 * GDN_HEADS * HEAD_DIM
    nz = GDN_HEADS * HEAD_DIM
    proj = rms_matmul(x, nmix, w_in, nqkv + nz, 512)
    ab = rms_matmul(x, nmix, _pad_cols(w_in[:, nqkv + nz:], 128), 128, 128)
    past8 = jnp.zeros((8, nqkv), F32)
    s0 = jnp.zeros((GDN_HEADS, HEAD_DIM, HEAD_DIM), F32)
    o, s_out = gdn_seq(proj, ab, conv_w, a_log, dt_bias, norm_w, past8, s0)
    x1 = matmul_res(o, w_out, x)
    return x1, s_out, proj[-(GDN_CONV - 1):, :nqkv]


def ffn_layer_seq(x, nw, wg, wu, cw, cb, wd, final_nw=None):
    act, st = ffn_up_seq(x, nw, wg, wu, cw, cb)
    return matmul_res(act, wd, x, final_nw=final_nw), st


def ffn_layer_step(x, nw, wg, wu, cw, cb, wd, past, final_nw=None):
    act, st = ffn_up_step(x, nw, wg, wu, cw, cb, past)
    return matmul_res(act, wd, x, final_nw=final_nw), st


KV_SECTION = NSA_GROUPS * HEAD_DIM
KV_ROTARY_SECTIONS = (2, 4)


def _rope_tables(pos):
    half = HEAD_DIM // 2
    inv = jnp.float32(ROPE_THETA) ** (-jnp.arange(half, dtype=F32) / half)
    ang = pos.astype(F32)[:, None] * inv[None, :]
    cos, sin = jnp.cos(ang), jnp.sin(ang)
    return jnp.concatenate([cos, cos], axis=1), jnp.concatenate([-sin, sin], axis=1)


def _rope(x, cosf, sinf):
    return x * cosf + pltpu.roll(x, HEAD_DIM // 2, 1) * sinf


def _kv_rows_kernel(x_ref, nw_ref, w_ref, cos_ref, sin_ref, o_ref, ob_ref, h_ref):
    j = pl.program_id(1)

    @pl.when(j == 0)
    def _():
        h_ref[...] = _bf(_rms(x_ref[...], nw_ref[...]))

    y = jnp.dot(h_ref[...], _bf(w_ref[...]), preferred_element_type=F32)
    is_rot = (j == KV_ROTARY_SECTIONS[0]) | (j == KV_ROTARY_SECTIONS[1])

    @pl.when(is_rot)
    def _():
        cosf, sinf = cos_ref[...], sin_ref[...]
        for g in range(NSA_GROUPS):
            sl = slice(g * HEAD_DIM, (g + 1) * HEAD_DIM)
            yg = _rope(y[:, sl], cosf, sinf)
            o_ref[:, sl] = yg
            ob_ref[:, sl] = _bf(yg)

    @pl.when(jnp.logical_not(is_rot))
    def _():
        o_ref[...] = y
        ob_ref[...] = _bf(y)


def kv_rows(x, nw, w, cosf, sinf, tm=1024):
    m, k = x.shape
    n = w.shape[1]
    tm = min(tm, m)
    return pl.pallas_call(
        _kv_rows_kernel,
        name="kv_rows",
        grid=(m // tm, n // KV_SECTION),
        in_specs=[
            pl.BlockSpec((tm, k), lambda i, j: (i, 0)),
            pl.BlockSpec((1, k), lambda i, j: (0, 0)),
            pl.BlockSpec((k, KV_SECTION), lambda i, j: (0, j)),
            pl.BlockSpec((tm, HEAD_DIM), lambda i, j: (i, 0)),
            pl.BlockSpec((tm, HEAD_DIM), lambda i, j: (i, 0)),
        ],
        out_specs=[pl.BlockSpec((tm, KV_SECTION), lambda i, j: (i, j)),
                   pl.BlockSpec((tm, KV_SECTION), lambda i, j: (i, j))],
        out_shape=[jax.ShapeDtypeStruct((m, n), F32), jax.ShapeDtypeStruct((m, n), BF16)],
        scratch_shapes=[pltpu.VMEM((tm, k), BF16)],
        compiler_params=_cparams("parallel", "arbitrary"),
    )(x, nw.reshape(1, k), w, cosf, sinf)


def _compress_seq_kernel(rows_ref, pe_ref, w1_ref, w2_ref, o_ref, xs_ref, ps_ref):
    nseg = o_ref.shape[2]
    half = CMP_STRIDE * HEAD_DIM
    ps_ref[...] = jnp.zeros(ps_ref.shape, F32)
    for t in range(CMP_STRIDE):
        sl = slice(t * HEAD_DIM, (t + 1) * HEAD_DIM)
        xs_ref[:, sl] = _bf(rows_ref[pl.ds(t, nseg, stride=CMP_STRIDE), :])
        ps_ref[0:1, sl] = pe_ref[0, t:t + 1, :]
        ps_ref[8:9, sl] = pe_ref[0, CMP_STRIDE + t:CMP_STRIDE + t + 1, :]
    xs = xs_ref[...]
    w_lo = _bf(w1_ref[0, 0:half, :])
    w_hi = _bf(w1_ref[0, half:2 * half, :])
    lo = jnp.dot(xs, w_lo, preferred_element_type=F32) + _dot(ps_ref[0:8, :], w_lo)[0:1]
    hi = jnp.dot(xs, w_hi, preferred_element_type=F32) + _dot(ps_ref[8:16, :], w_hi)[0:1]
    hid = _silu(lo + pltpu.roll(hi, nseg - 1, 0))
    o_ref[0, 0] = _bf(_dot(hid, w2_ref[0]))


def compress_seq(kv, pe, w1, w2):
    t = kv.shape[0]
    nseg = t // CMP_STRIDE
    hid = w1.shape[2]
    return pl.pallas_call(
        _compress_seq_kernel,
        name="compress_seq",
        grid=(2, NSA_GROUPS),
        in_specs=[
            pl.BlockSpec((t, HEAD_DIM), lambda a, g: (0, a * NSA_GROUPS + g)),
            pl.BlockSpec((1, CMP_BLOCK, HEAD_DIM), lambda a, g: (a, 0, 0)),
            pl.BlockSpec((1, CMP_BLOCK * HEAD_DIM, hid), lambda a, g: (a, 0, 0)),
            pl.BlockSpec((1, hid, HEAD_DIM), lambda a, g: (a, 0, 0)),
        ],
        out_specs=pl.BlockSpec((1, 1, nseg, HEAD_DIM), lambda a, g: (a, g, 0, 0)),
        out_shape=jax.ShapeDtypeStruct((2, NSA_GROUPS, nseg, HEAD_DIM), BF16),
        scratch_shapes=[pltpu.VMEM((nseg, CMP_STRIDE * HEAD_DIM), BF16),
                        pltpu.VMEM((16, CMP_STRIDE * HEAD_DIM), F32)],
        compiler_params=_cparams("arbitrary", "arbitrary"),
    )(kv, pe, w1, w2)


def _group_rows(ref, g):
    return jnp.concatenate([ref[:, (NSA_HPG * g + h) * HEAD_DIM:(NSA_HPG * g + h + 1) * HEAD_DIM]
                            for h in range(NSA_HPG)], axis=0)


def _masked_softmax(s, mask, exp=jnp.exp):
    s = jnp.where(mask, s, NEG_BIG)
    m = jnp.max(s, axis=1, keepdims=True)
    e = jnp.where(mask, exp(s - m), 0.0)
    return e / jnp.maximum(jnp.sum(e, axis=1, keepdims=True), 1e-30)


def _split3(x):
    hi = _bf(x)
    r = x - hi.astype(F32)
    mid = _bf(r)
    return hi, mid, _bf(r - mid.astype(F32))


def _select_blocks(score, blk, n_pick):
    sel = jnp.zeros(score.shape, F32)
    for _ in range(n_pick):
        mx = jnp.max(score, axis=1, keepdims=True)
        idx = jnp.min(jnp.where(score == mx, blk, 1e9), axis=1, keepdims=True)
        pick = blk == idx
        sel = jnp.where(pick, 1.0, sel)
        score = jnp.where(pick, -3.0, score)
    return sel


def _nsa_cmp_kernel(q_ref, cos_ref, sin_ref, kc_ref, vc_ref, oc_ref, sel_ref, qrot_ref, *, n_sel):
    i = pl.program_id(0)
    Q = Q_BLOCK
    nc = kc_ref.shape[1]
    scale = HEAD_DIM ** -0.5
    cosf, sinf = cos_ref[...], sin_ref[...]
    for h in range(NSA_HEADS):
        sl = slice(h * HEAD_DIM, (h + 1) * HEAD_DIM)
        qrot_ref[:, sl] = _bf(_rope(q_ref[:, sl], cosf, sinf) * (scale * LOG2E))

    qpos_r = i * Q + lax.broadcasted_iota(jnp.int32, (NSA_HPG * Q, nc), 0) % Q
    c_end = lax.broadcasted_iota(jnp.int32, (NSA_HPG * Q, nc), 1) * CMP_STRIDE + (CMP_BLOCK - 1)
    cmask = c_end <= qpos_r
    ci = lax.broadcasted_iota(jnp.int32, (nc, 128), 0) * CMP_STRIDE
    sj = lax.broadcasted_iota(jnp.int32, (nc, 128), 1) * SEL_BLOCK
    cover = jnp.where((ci < sj + SEL_BLOCK) & (ci + CMP_BLOCK > sj), 1.0, 0.0).astype(BF16)
    qpos = i * Q + lax.broadcasted_iota(jnp.int32, (Q, 128), 0)
    blk = lax.broadcasted_iota(jnp.int32, (Q, 128), 1)
    valid = blk * SEL_BLOCK <= qpos
    force = (blk == 0) | (blk >= qpos // SEL_BLOCK - 1)
    blkf = blk.astype(F32)

    for g in range(NSA_GROUPS):
        p = _masked_softmax(_dot_nt(_group_rows(q_ref, g), kc_ref[g]) * scale, cmask)
        oc = _dot(p, vc_ref[g])
        for h in range(NSA_HPG):
            oc_ref[:, (NSA_HPG * g + h) * HEAD_DIM:(NSA_HPG * g + h + 1) * HEAD_DIM] = oc[h * Q:(h + 1) * Q]
        ps = p[0:Q] + p[Q:2 * Q] + p[2 * Q:3 * Q] + p[3 * Q:4 * Q]
        imp = sum(jnp.dot(part, cover, preferred_element_type=F32) for part in _split3(ps))
        score = jnp.where(valid, jnp.where(force, 1e9, imp), -1.0)
        score = jnp.where(blk < n_sel, score, -2.0)
        picked = _select_blocks(score, blkf, N_SELECT)
        sel_ref[:, g * 128:(g + 1) * 128] = _bf(jnp.where(picked > 0.5, 0.0, NEG_BIG))


def nsa_cmp(q, cosf, sinf, kc, vc):
    t, d = q.shape
    n_sel = t // SEL_BLOCK
    assert N_SELECT <= n_sel <= 128 and t % Q_BLOCK == 0
    nc = kc.shape[1]
    row = lambda w: pl.BlockSpec((Q_BLOCK, w), lambda i: (i, 0))
    full = pl.BlockSpec((NSA_GROUPS, nc, HEAD_DIM), lambda i: (0, 0, 0))
    return pl.pallas_call(
        functools.partial(_nsa_cmp_kernel, n_sel=n_sel),
        name="nsa_cmp",
        grid=(t // Q_BLOCK,),
        in_specs=[row(d), row(HEAD_DIM), row(HEAD_DIM), full, full],
        out_specs=[row(d), row(NSA_GROUPS * 128), row(d)],
        out_shape=[jax.ShapeDtypeStruct((t, d), F32), jax.ShapeDtypeStruct((t, NSA_GROUPS * 128), BF16),
                   jax.ShapeDtypeStruct((t, d), BF16)],
        compiler_params=_cparams("parallel"),
    )(q, cosf, sinf, kc, vc)


SEL_KEYS = 512


def _nsa_sel_kernel(q_ref, sel_ref, k_ref, v_ref, o_ref, m_scr, l_scr, acc_scr):
    i, kt = pl.program_id(0), pl.program_id(1)
    Q = Q_BLOCK
    last = (i * Q + Q - 1) // SEL_KEYS

    @pl.when(kt == 0)
    def _():
        m_scr[...] = jnp.full(m_scr.shape, NEG_BIG, F32)
        l_scr[...] = jnp.zeros(l_scr.shape, F32)
        acc_scr[...] = jnp.zeros(acc_scr.shape, F32)

    def tile(diagonal):
        brow = lax.broadcasted_iota(jnp.int32, (128, SEL_KEYS), 0)
        kcol = lax.broadcasted_iota(jnp.int32, (128, SEL_KEYS), 1)
        expand = jnp.where(brow == kt * (SEL_KEYS // SEL_BLOCK) + kcol // SEL_BLOCK, 1.0, 0.0).astype(BF16)
        if diagonal:
            qpos = i * Q + lax.broadcasted_iota(jnp.int32, (Q, SEL_KEYS), 0)
            kpos = kt * SEL_KEYS + lax.broadcasted_iota(jnp.int32, (Q, SEL_KEYS), 1)
            causal = kpos <= qpos
        gs = range(NSA_GROUPS)
        sl = [slice(g * HEAD_DIM, (g + 1) * HEAD_DIM) for g in gs]
        bias = [jnp.dot(sel_ref[:, g * 128:(g + 1) * 128], expand, preferred_element_type=F32) for g in gs]
        if diagonal:
            bias = [jnp.where(causal, b, NEG_BIG) for b in bias]
        s = [_dot_nt(_group_rows(q_ref, g), k_ref[:, sl[g]]) for g in gs]
        s = [(s[g].reshape(NSA_HPG, Q, SEL_KEYS) + bias[g][None]).reshape(NSA_HPG * Q, SEL_KEYS) for g in gs]
        cols = [slice(c * 128, (c + 1) * 128) for c in range(SEL_KEYS // 128)]
        rep = lambda x: jnp.broadcast_to(x, (NSA_HPG * Q, 128))
        m_old = [m_scr[g] for g in gs]
        m_new = [jnp.maximum(m_old[g], rep(jnp.max(s[g], axis=1, keepdims=True))) for g in gs]
        e = [jnp.concatenate([jnp.exp2(s[g][:, c] - m_new[g]) for c in cols], axis=1) for g in gs]
        pv = [_dot(e[g], v_ref[:, sl[g]]) for g in gs]
        for g in gs:
            alpha = jnp.exp2(m_old[g] - m_new[g])
            l_scr[g] = alpha * l_scr[g] + rep(jnp.sum(e[g], axis=1, keepdims=True))
            acc_scr[g] = alpha * acc_scr[g] + pv[g]
            m_scr[g] = m_new[g]

    pl.when(kt < last)(functools.partial(tile, False))
    pl.when(kt == last)(functools.partial(tile, True))

    @pl.when(kt == pl.num_programs(1) - 1)
    def _():
        for g in range(NSA_GROUPS):
            o = acc_scr[g] / jnp.maximum(l_scr[g], 1e-30)
            for h in range(NSA_HPG):
                o_ref[:, (NSA_HPG * g + h) * HEAD_DIM:(NSA_HPG * g + h + 1) * HEAD_DIM] = o[h * Q:(h + 1) * Q]


def nsa_sel(qrot, sel, kvb):
    t, d = qrot.shape
    nkt = t // SEL_KEYS
    kblk = lambda sec: pl.BlockSpec(
        (SEL_KEYS, KV_SECTION), lambda i, kt: (jnp.minimum(kt, (i * Q_BLOCK + Q_BLOCK - 1) // SEL_KEYS), sec))
    return pl.pallas_call(
        _nsa_sel_kernel,
        name="nsa_sel",
        grid=(t // Q_BLOCK, nkt),
        in_specs=[pl.BlockSpec((Q_BLOCK, d), lambda i, kt: (i, 0)),
                  pl.BlockSpec((Q_BLOCK, NSA_GROUPS * 128), lambda i, kt: (i, 0)),
                  kblk(2), kblk(3)],
        out_specs=pl.BlockSpec((Q_BLOCK, d), lambda i, kt: (i, 0)),
        out_shape=jax.ShapeDtypeStruct((t, d), F32),
        scratch_shapes=[pltpu.VMEM((NSA_GROUPS, NSA_HPG * Q_BLOCK, 128), F32),
                        pltpu.VMEM((NSA_GROUPS, NSA_HPG * Q_BLOCK, 128), F32),
                        pltpu.VMEM((NSA_GROUPS, NSA_HPG * Q_BLOCK, HEAD_DIM), F32)],
        compiler_params=_cparams("parallel", "arbitrary"),
    )(qrot, sel, kvb, kvb)


WIN_BLOCKS = WINDOW // Q_BLOCK + 1


def _nsa_win_kernel(q_ref, *refs):
    k_refs, v_refs = refs[:WIN_BLOCKS], refs[WIN_BLOCKS:2 * WIN_BLOCKS]
    oc_ref, os_ref, gate_ref, o_ref = refs[2 * WIN_BLOCKS:]
    i = pl.program_id(0)
    Q = Q_BLOCK
    span = WIN_BLOCKS * Q
    kw = jnp.concatenate([r[...] for r in k_refs], axis=0)
    vw = jnp.concatenate([r[...] for r in v_refs], axis=0)
    qpos = i * Q + lax.broadcasted_iota(jnp.int32, (NSA_HPG * Q, span), 0) % Q
    kpos = (i - (WIN_BLOCKS - 1)) * Q + lax.broadcasted_iota(jnp.int32, (NSA_HPG * Q, span), 1)
    mask = (kpos <= qpos) & (kpos > qpos - WINDOW) & (kpos >= 0)
    gates = _sigmoid(gate_ref[...])
    for g in range(NSA_GROUPS):
        sl = slice(g * HEAD_DIM, (g + 1) * HEAD_DIM)
        p = _masked_softmax(_dot_nt(_group_rows(q_ref, g), kw[:, sl]), mask, exp=jnp.exp2)
        ow = _dot(p, vw[:, sl])
        for h in range(NSA_HPG):
            head = NSA_HPG * g + h
            hs = slice(head * HEAD_DIM, (head + 1) * HEAD_DIM)
            o = (gates[:, 3 * head:3 * head + 1] * oc_ref[:, hs] + gates[:, 3 * head + 1:3 * head + 2] * os_ref[:, hs]
                 + gates[:, 3 * head + 2:3 * head + 3] * ow[h * Q:(h + 1) * Q])
            o_ref[:, hs] = _bf(o)


def nsa_win(qrot, kvb, oc, osel, gates):
    t, d = qrot.shape
    row = lambda w: pl.BlockSpec((Q_BLOCK, w), lambda i: (i, 0))
    kblk = lambda sec, b: pl.BlockSpec(
        (Q_BLOCK, KV_SECTION), lambda i: (jnp.maximum(i - (WIN_BLOCKS - 1) + b, 0), sec))
    return pl.pallas_call(
        _nsa_win_kernel,
        name="nsa_win",
        grid=(t // Q_BLOCK,),
        in_specs=([row(d)] + [kblk(4, b) for b in range(WIN_BLOCKS)] + [kblk(5, b) for b in range(WIN_BLOCKS)]
                  + [row(d), row(d), row(128)]),
        out_specs=row(d),
        out_shape=jax.ShapeDtypeStruct((t, d), BF16),
        compiler_params=_cparams("parallel"),
    )(qrot, *([kvb] * (2 * WIN_BLOCKS)), oc, osel, gates)


def nsa_layer_seq(x, nmix, w_q, w_out, kvb, kc, vc, cosf, sinf):
    nq = NSA_HEADS * HEAD_DIM
    q = rms_matmul(x, nmix, w_q, nq, 512)
    gates = rms_matmul(x, nmix, _pad_cols(w_q[:, nq:], 128), 128, 128)
    oc, sel, qrot = nsa_cmp(q, cosf, sinf, kc, vc)
    osel = nsa_sel(qrot, sel, kvb)
    o = nsa_win(qrot, kvb, oc, osel, gates)
    return matmul_res(o, w_out, x)


def _head_column(row, offset, n_heads):
    h = lax.broadcasted_iota(jnp.int32, (n_heads, 128), 0)
    lane = lax.broadcasted_iota(jnp.int32, (n_heads, 128), 1)
    return jnp.sum(jnp.where(lane == h + offset, row, 0.0), axis=1, keepdims=True)


def _pad_rows(rows, n):
    return jnp.concatenate(rows + [jnp.zeros((n - len(rows), rows[0].shape[1]), F32)], axis=0)


def _gdn_step_kernel(proj_ref, ab_ref, past_ref, cw_ref, alog_ref, dtb_ref, nw_ref, s_ref, o_ref, sout_ref):
    H, D = GDN_HEADS, HEAD_DIM
    proj = proj_ref[0]
    cw = cw_ref[...]
    y = cw[GDN_CONV - 1] * proj[0:3 * H]
    for j in range(GDN_CONV - 1):
        y = y + cw[j] * past_ref[0, j]
    y = _silu(y)
    q = _l2norm(y[0:H]) * (D ** -0.5)
    k = _l2norm(y[H:2 * H])
    v = y[2 * H:3 * H]
    ab = ab_ref[0]
    g_row = -jnp.exp(alog_ref[...]) * _softplus(ab + dtb_ref[...])
    eg = jnp.exp(_head_column(g_row, 0, H))
    beta = _head_column(_sigmoid(ab), H, H)
    kb = k * beta
    w = kb * eg
    qe = q * eg
    rs = [_dot(_pad_rows([w[h:h + 1], qe[h:h + 1]], 8), s_ref[0, h]) for h in range(H)]
    ws = jnp.concatenate([r[0:1] for r in rs], axis=0)
    qs = jnp.concatenate([r[1:2] for r in rs], axis=0)
    v_new = v * beta - ws
    o = qs + jnp.sum(q * k, axis=1, keepdims=True) * v_new
    for h in range(H):
        outer = _dot_tn(_pad_rows([k[h:h + 1]], 8), _pad_rows([v_new[h:h + 1]], 8))
        sout_ref[0, h] = s_ref[0, h] * eg[h:h + 1] + outer
    o_ref[0] = _bf(_rms(o, nw_ref[...]) * _silu(proj[3 * H:4 * H]))


def gdn_step(proj, ab, conv_past, conv_w, a_log, dt_bias, norm_w, s0):
    b = proj.shape[0]
    H, D = GDN_HEADS, HEAD_DIM
    pad16 = lambda a: jnp.pad(a.reshape(1, H), ((0, 0), (0, 128 - H)))
    o, s_out = pl.pallas_call(
        _gdn_step_kernel,
        name="gdn_step",
        grid=(b,),
        in_specs=[pl.BlockSpec((1, 4 * H, D), lambda i: (i, 0, 0)),
                  pl.BlockSpec((1, 1, 128), lambda i: (i, 0, 0)),
                  pl.BlockSpec((1, GDN_CONV - 1, 3 * H, D), lambda i: (i, 0, 0, 0)),
                  pl.BlockSpec((GDN_CONV, 3 * H, D), lambda i: (0, 0, 0)),
                  pl.BlockSpec((1, 128), lambda i: (0, 0)),
                  pl.BlockSpec((1, 128), lambda i: (0, 0)),
                  pl.BlockSpec((1, D), lambda i: (0, 0)),
                  pl.BlockSpec((1, H, D, D), lambda i: (i, 0, 0, 0))],
        out_specs=[pl.BlockSpec((1, H, D), lambda i: (i, 0, 0)),
                   pl.BlockSpec((1, H, D, D), lambda i: (i, 0, 0, 0))],
        out_shape=[jax.ShapeDtypeStruct((b, H, D), BF16), jax.ShapeDtypeStruct((b, H, D, D), F32)],
        compiler_params=_cparams("parallel"),
    )(proj.reshape(b, 4 * H, D), ab.reshape(b, 1, 128), conv_past.reshape(b, GDN_CONV - 1, 3 * H, D),
      conv_w.reshape(GDN_CONV, 3 * H, D), pad16(a_log), pad16(dt_bias), norm_w.reshape(1, D), s0)
    return o.reshape(b, H * D), s_out


def gdn_layer_step(x, nmix, w_in, conv_w, a_log, dt_bias, norm_w, w_out, conv_past, s0):
    nqkv = 3 * GDN_HEADS * HEAD_DIM
    nz = GDN_HEADS * HEAD_DIM
    proj = rms_matmul(x, nmix, w_in, nqkv + nz, 512)
    ab = rms_matmul(x, nmix, _pad_cols(w_in[:, nqkv + nz:], 128), 128, 128)
    o, s_out = gdn_step(proj, ab, conv_past, conv_w, a_log, dt_bias, norm_w, s0)
    conv_new = jnp.concatenate([conv_past[:, 1:], proj[:, None, :nqkv]], axis=1)
    return matmul_res(o, w_out, x), s_out, conv_new


def _compress_pages_kernel(pt_ref, *refs, n_pages):
    page_refs = refs[:n_pages]
    pe_ref, w1_ref, w2_ref, o_ref, xs_ref, ps_ref = refs[n_pages:]
    G, D = NSA_GROUPS, HEAD_DIM
    seg_pp = page_refs[0].shape[0] // (2 * G * CMP_STRIDE)
    nseg = n_pages * seg_pp
    half = CMP_STRIDE * D
    ps_ref[...] = jnp.zeros(ps_ref.shape, F32)
    for a in range(2):
        for t in range(CMP_STRIDE):
            sl = slice(t * D, (t + 1) * D)
            ps_ref[a, 0:1, sl] = pe_ref[a, t:t + 1, :]
            ps_ref[a, 8:9, sl] = pe_ref[a, CMP_STRIDE + t:CMP_STRIDE + t + 1, :]
    for p in range(n_pages):
        for t in range(CMP_STRIDE):
            for a in range(2):
                for g in range(G):
                    xs_ref[a, g * nseg + p * seg_pp:g * nseg + (p + 1) * seg_pp, t * D:(t + 1) * D] = (
                        page_refs[p][pl.ds(t * 2 * G + a * G + g, seg_pp, stride=CMP_STRIDE * 2 * G), :])
    for a in range(2):
        xs = _bf(xs_ref[a])
        w_lo = w1_ref[a, 0:half, :]
        w_hi = w1_ref[a, half:2 * half, :]
        lo = jnp.dot(xs, w_lo, preferred_element_type=F32) + _dot(ps_ref[a, 0:8, :], w_lo)[0:1]
        hi = jnp.dot(xs, w_hi, preferred_element_type=F32) + _dot(ps_ref[a, 8:16, :], w_hi)[0:1]
        hi = jnp.concatenate([pltpu.roll(hi[g * nseg:(g + 1) * nseg], nseg - 1, 0) for g in range(G)], axis=0)
        o_ref[0, a] = _bf(_dot(_silu(lo + hi), w2_ref[a]))


def compress_pages(cache, page_table, pe, w1b, w2):
    b, n_pages = page_table.shape
    page = cache.shape[1]
    seg_pp = page // CMP_STRIDE
    nseg = n_pages * seg_pp
    hid = w1b.shape[2]
    rows_pp = page * 2 * NSA_GROUPS
    cache = cache.reshape(cache.shape[0] * rows_pp, HEAD_DIM)

    def page_spec(p):
        return pl.BlockSpec((rows_pp, HEAD_DIM), lambda i, pt: (pt[i, p], 0))

    const = lambda shape: pl.BlockSpec(shape, lambda i, pt: (0,) * len(shape))
    return pl.pallas_call(
        functools.partial(_compress_pages_kernel, n_pages=n_pages),
        name="compress_pages",
        grid_spec=pltpu.PrefetchScalarGridSpec(
            num_scalar_prefetch=1,
            grid=(b,),
            in_specs=[page_spec(p) for p in range(n_pages)] + [
                const((2, CMP_BLOCK, HEAD_DIM)), const((2, CMP_BLOCK * HEAD_DIM, hid)), const((2, hid, HEAD_DIM))],
            out_specs=pl.BlockSpec((1, 2, NSA_GROUPS * nseg, HEAD_DIM), lambda i, pt: (i, 0, 0, 0)),
            scratch_shapes=[pltpu.VMEM((2, NSA_GROUPS * nseg, CMP_STRIDE * HEAD_DIM), F32),
                            pltpu.VMEM((2, 16, CMP_STRIDE * HEAD_DIM), F32)],
        ),
        out_shape=jax.ShapeDtypeStruct((b, 2, NSA_GROUPS * nseg, HEAD_DIM), BF16),
        compiler_params=_cparams("parallel"),
    )(page_table, *([cache] * n_pages), pe, w1b, w2)


def _head_rows_of_group(x, g):
    h = lax.broadcasted_iota(jnp.int32, x.shape, 0)
    return jnp.where(h // NSA_HPG == g, x, 0.0)


def _nsa_step_kernel(pt_ref, *refs, n_pages):
    page_refs = refs[:n_pages]
    q_ref, gate_ref, cos_ref, sin_ref, kvc_ref, new_ref, win_ref, o_ref, nwin_ref = refs[n_pages:]
    H, G, D = NSA_HEADS, NSA_GROUPS, HEAD_DIM
    page = page_refs[0].shape[0] // (2 * G)
    past = n_pages * page
    nseg = past // CMP_STRIDE
    n_sel = past // SEL_BLOCK + 1
    wb = win_ref.shape[0] // (2 * G)
    scale = D ** -0.5
    q_raw = q_ref[0]
    q_rot = _rope(q_raw, cos_ref[...], sin_ref[...])
    nwin_ref[0:(wb - 1) * 2 * G, :] = win_ref[2 * G:wb * 2 * G, :]
    nwin_ref[(wb - 1) * 2 * G:wb * 2 * G, :] = new_ref[4 * G:6 * G, :]

    def token_rows(ref, n, kv, g):
        return ref[pl.ds(kv * G + g, n, stride=2 * G), :]

    lane = lax.broadcasted_iota(jnp.int32, (8, 128), 1)
    blkf = lane.astype(F32)
    c_ok = lax.broadcasted_iota(jnp.int32, (H, nseg), 1) * CMP_STRIDE + (CMP_BLOCK - 1) <= past
    ci = lax.broadcasted_iota(jnp.int32, (nseg, 128), 0) * CMP_STRIDE
    sj = lax.broadcasted_iota(jnp.int32, (nseg, 128), 1) * SEL_BLOCK
    cover = jnp.where((ci < sj + SEL_BLOCK) & (ci + CMP_BLOCK > sj), 1.0, 0.0).astype(BF16)
    valid = lane * SEL_BLOCK <= past
    force = (lane == 0) | (lane >= past // SEL_BLOCK - 1)
    brow = lax.broadcasted_iota(jnp.int32, (128, past), 0)
    kcol = lax.broadcasted_iota(jnp.int32, (128, past), 1)
    expand = jnp.where(brow == kcol // SEL_BLOCK, 1.0, 0.0).astype(BF16)
    all_keys = jnp.full((H, wb), True)

    o_cmp = jnp.zeros((H, D), F32)
    ps_rows = []
    for g in range(G):
        p = _masked_softmax(_dot_nt(q_raw, kvc_ref[0, 0, g * nseg:(g + 1) * nseg, :]) * scale, c_ok)
        o_cmp = o_cmp + _head_rows_of_group(_dot(p, kvc_ref[0, 1, g * nseg:(g + 1) * nseg, :]), g)
        ps_rows.append(jnp.sum(_head_rows_of_group(p, g), axis=0, keepdims=True))
    imp = sum(jnp.dot(part, cover, preferred_element_type=F32) for part in _split3(_pad_rows(ps_rows, 8)))
    score = jnp.where(valid, jnp.where(force, 1e9, imp), -1.0)
    score = jnp.where(lane < n_sel, score, -2.0)
    sel = _select_blocks(score, blkf, N_SELECT)
    picked = jnp.dot(_bf(sel), expand, preferred_element_type=F32)

    o_sel = jnp.zeros((H, D), F32)
    o_win = jnp.zeros((H, D), F32)
    for g in range(G):
        kmask = jnp.broadcast_to(picked[g:g + 1], (H, past)) > 0.5
        kg = jnp.concatenate([_bf(token_rows(r, page, 0, g)) for r in page_refs], axis=0)
        vg = jnp.concatenate([_bf(token_rows(r, page, 1, g)) for r in page_refs], axis=0)
        s = jnp.where(kmask, _dot_nt(q_rot, kg) * scale, NEG_BIG)
        k_new = new_ref[2 * G + g:2 * G + g + 1, :]
        v_new = new_ref[3 * G + g:3 * G + g + 1, :]
        new_ok = sel[g:g + 1, n_sel - 1:n_sel] > 0.5
        s_new = jnp.where(new_ok, jnp.sum(q_rot * k_new, axis=1, keepdims=True) * scale, NEG_BIG)
        m = jnp.maximum(jnp.max(s, axis=1, keepdims=True), s_new)
        e = jnp.where(kmask, jnp.exp(s - m), 0.0)
        e_new = jnp.where(new_ok, jnp.exp(s_new - m), 0.0)
        den = jnp.maximum(jnp.sum(e, axis=1, keepdims=True) + e_new, 1e-30)
        o_sel = o_sel + _head_rows_of_group((_dot(e, vg) + e_new * v_new) / den, g)
        pw = _masked_softmax(_dot_nt(q_rot, token_rows(nwin_ref, wb, 0, g)) * scale, all_keys)
        o_win = o_win + _head_rows_of_group(_dot(pw, token_rows(nwin_ref, wb, 1, g)), g)

    gates = _sigmoid(gate_ref[0])
    hh = lax.broadcasted_iota(jnp.int32, (H, 128), 0)
    ll = lax.broadcasted_iota(jnp.int32, (H, 128), 1)
    gcol = lambda c: jnp.sum(jnp.where(ll == 3 * hh + c, gates, 0.0), axis=1, keepdims=True)
    o_ref[0] = _bf(gcol(0) * o_cmp + gcol(1) * o_sel + gcol(2) * o_win)


def nsa_step(q, gates, cosf, sinf, kvc, kv_new, cache_sel, cache_win, page_table):
    b, n_pages = page_table.shape
    page = cache_sel.shape[1]
    wb = cache_win.shape[1]
    H, G, D = NSA_HEADS, NSA_GROUPS, HEAD_DIM

    rows_pp, rows_w = page * 2 * G, wb * 2 * G
    cache_sel = cache_sel.reshape(cache_sel.shape[0] * rows_pp, D)
    cache_win2 = cache_win.reshape(b * rows_w, D)

    def page_spec(p):
        return pl.BlockSpec((rows_pp, D), lambda i, pt: (pt[i, p], 0))

    per_seq = lambda *shape: pl.BlockSpec((1,) + shape, lambda i, pt: (i,) + (0,) * len(shape))
    flat_seq = lambda rows: pl.BlockSpec((rows, D), lambda i, pt: (i, 0))
    const = lambda *shape: pl.BlockSpec(shape, lambda i, pt: (0,) * len(shape))
    o, nwin = pl.pallas_call(
        functools.partial(_nsa_step_kernel, n_pages=n_pages),
        name="nsa_step",
        grid_spec=pltpu.PrefetchScalarGridSpec(
            num_scalar_prefetch=1,
            grid=(b,),
            in_specs=[page_spec(p) for p in range(n_pages)] + [
                per_seq(H, D), per_seq(1, 128), const(1, D), const(1, D),
                per_seq(2, kvc.shape[2], D), flat_seq(6 * G), flat_seq(rows_w)],
            out_specs=[per_seq(H, D), flat_seq(rows_w)],
        ),
        out_shape=[jax.ShapeDtypeStruct((b, H, D), BF16), jax.ShapeDtypeStruct((b * rows_w, D), F32)],
        compiler_params=_cparams("parallel"),
    )(page_table, *([cache_sel] * n_pages), q.reshape(b, H, D), gates.reshape(b, 1, 128), cosf, sinf,
      kvc, kv_new.reshape(b * 6 * G, D), cache_win2)
    return o.reshape(b, H * D), nwin.reshape(cache_win.shape)


def kernel(x_prompt, x_sample, state_gdn, state_gdn_conv, state_ffn_conv, cache_cmp_kv, cache_sel_kv, cache_win_kv, page_table, norm_mixer, norm_ffn, norm_kv, norm_final, gdn_w_in, gdn_conv_w, gdn_A_log, gdn_dt_bias, gdn_norm_w, gdn_w_out, nsa_w_q, nsa_w_out, kv_w, cmp_pe_k, cmp_pe_v, cmp_w1_k, cmp_w2_k, cmp_w1_v, cmp_w2_v, ffn_w_gate, ffn_w_up, ffn_conv_w, ffn_conv_b, ffn_w_down):
    cmp_pe = jnp.stack([cmp_pe_k, cmp_pe_v])
    cmp_w1 = jnp.stack([cmp_w1_k, cmp_w1_v])
    cmp_w2 = jnp.stack([cmp_w2_k, cmp_w2_v])

    x = x_prompt[0]
    t = x.shape[0]
    x, gdn_s_p, gdn_c_p = gdn_layer_seq(x, norm_mixer[0], gdn_w_in[0], gdn_conv_w[0], gdn_A_log[0], gdn_dt_bias[0],
                                        gdn_norm_w[0], gdn_w_out[0])
    x, ffn_c0_p = ffn_layer_seq(x, norm_ffn[0], ffn_w_gate[0], ffn_w_up[0], ffn_conv_w[0], ffn_conv_b[0],
                                ffn_w_down[0])
    cosf, sinf = _rope_tables(jnp.arange(t, dtype=jnp.int32))
    kv, kvb = kv_rows(x, norm_kv, kv_w, cosf, sinf)
    kvc = compress_seq(kv, cmp_pe, cmp_w1, cmp_w2)
    x = nsa_layer_seq(x, norm_mixer[1], nsa_w_q[0], nsa_w_out[0], kvb, kvc[0], kvc[1], cosf, sinf)
    y_p, ffn_c1_p = ffn_layer_seq(x, norm_ffn[1], ffn_w_gate[1], ffn_w_up[1], ffn_conv_w[1], ffn_conv_b[1],
                                  ffn_w_down[1], final_nw=norm_final)
    g, hd = NSA_GROUPS, HEAD_DIM
    nrow = 2 * g * hd
    cmp_p = kv[:, :nrow].reshape(1, t, 2, g, hd)
    sel_p = kv[:, nrow:2 * nrow].reshape(1, t, 2, g, hd)
    win_p = kv[t - min(WINDOW, t):, 2 * nrow:].reshape(1, min(WINDOW, t), 2, g, hd)
    assert x_sample.shape[1] == 1
    xs = x_sample[:, 0]
    b = xs.shape[0]
    n_pool, page = cache_sel_kv.shape[:2]
    past_len = page_table.shape[1] * page
    wb = cache_win_kv.shape[1]
    xs, gdn_s_s, gdn_c_s = gdn_layer_step(xs, norm_mixer[0], gdn_w_in[0], gdn_conv_w[0], gdn_A_log[0], gdn_dt_bias[0],
                                          gdn_norm_w[0], gdn_w_out[0], state_gdn_conv[0], state_gdn[0])
    xs, ffn_c0_s = ffn_layer_step(xs, norm_ffn[0], ffn_w_gate[0], ffn_w_up[0], ffn_conv_w[0], ffn_conv_b[0],
                                  ffn_w_down[0], state_ffn_conv[0])
    cos1, sin1 = _rope_tables(jnp.full((1,), past_len, jnp.int32))
    kv_s, _ = kv_rows(xs, norm_kv, kv_w, jnp.broadcast_to(cos1, (b, hd)), jnp.broadcast_to(sin1, (b, hd)))
    kvc_s = compress_pages(cache_cmp_kv, page_table, cmp_pe, _bf(cmp_w1), cmp_w2)
    nq = NSA_HEADS * hd
    q_s = rms_matmul(xs, norm_mixer[1], nsa_w_q[0], nq, 512)
    gates_s = rms_matmul(xs, norm_mixer[1], _pad_cols(nsa_w_q[0][:, nq:], 128), 128, 128)
    o_s, win_s = nsa_step(q_s, gates_s, cos1, sin1, kvc_s, kv_s, cache_sel_kv, cache_win_kv, page_table)
    xs = matmul_res(o_s, nsa_w_out[0], xs)
    y_s, ffn_c1_s = ffn_layer_step(xs, norm_ffn[1], ffn_w_gate[1], ffn_w_up[1], ffn_conv_w[1], ffn_conv_b[1],
                                   ffn_w_down[1], state_ffn_conv[1], final_nw=norm_final)
    cmp_s = kv_s[:, :nrow].reshape(b, 1, 2, g, hd)
    sel_s = kv_s[:, nrow:2 * nrow].reshape(b, 1, 2, g, hd)

    return (y_p[None], y_s[:, None],
            gdn_s_p[None, None], gdn_c_p[None, None], jnp.stack([ffn_c0_p, ffn_c1_p])[:, None], cmp_p, sel_p, win_p,
            gdn_s_s[None], gdn_c_s[None], jnp.stack([ffn_c0_s, ffn_c1_s]), cmp_s, sel_s, win_s)
```

```python
import functools

import jax
import jax.numpy as jnp
import numpy as np
from jax import lax
from jax.experimental import pallas as pl
from jax.experimental.pallas import tpu as pltpu

F32 = jnp.float32
BF16 = jnp.bfloat16

RMS_EPS = 1e-6
ROPE_THETA = 10000.0
HEAD_DIM = 128
GDN_HEADS = 16
GDN_CONV = 4
GDN_CHUNK = 128
NSA_HEADS = 16
NSA_GROUPS = 4
NSA_HPG = NSA_HEADS // NSA_GROUPS
CMP_STRIDE = 16
CMP_BLOCK = 32
SEL_BLOCK = 64
N_SELECT = 16
WINDOW = 512
Q_BLOCK = 128
FFN_CONV = 3
NEG_BIG = -1e30
LOG2E = 1.4426950408889634

VMEM_LIMIT = 52 * 1024 * 1024


def _cparams(*sem):
    return pltpu.CompilerParams(dimension_semantics=sem, vmem_limit_bytes=VMEM_LIMIT)


def _bf(x):
    return x.astype(BF16)


def _dot(a, b):
    return jnp.dot(_bf(a), _bf(b), preferred_element_type=F32)


def _dot_nt(a, b):
    return lax.dot_general(_bf(a), _bf(b), (((1,), (1,)), ((), ())), preferred_element_type=F32)


def _dot_tn(a, b):
    return lax.dot_general(_bf(a), _bf(b), (((0,), (0,)), ((), ())), preferred_element_type=F32)


def _sigmoid(x):
    return 1.0 / (1.0 + jnp.exp(-x))


def _silu(x):
    return x * _sigmoid(x)


def _rms(x, w):
    return x * lax.rsqrt(jnp.mean(x * x, axis=-1, keepdims=True) + RMS_EPS) * w


def _layered(w):
    return w if isinstance(w, tuple) else (w[None], 0)


def _w_spec(block, index_map, layer):
    return pl.BlockSpec((None,) + block, lambda *idx: (layer,) + index_map(*idx))


def _rms_mm_kernel(x_ref, nw_ref, w_ref, o_ref, h_ref):
    @pl.when(pl.program_id(1) == 0)
    def _():
        h_ref[...] = _bf(_rms(x_ref[...], nw_ref[...]))

    o_ref[...] = jnp.dot(h_ref[...], _bf(w_ref[...]), preferred_element_type=F32).astype(o_ref.dtype)


def rms_matmul(x, nw, w, n_out, tn, tm=1024, out_dtype=F32):
    m, k = x.shape
    tm = min(tm, m)
    return pl.pallas_call(
        _rms_mm_kernel,
        name="rms_mm",
        grid=(m // tm, n_out // tn),
        in_specs=[
            pl.BlockSpec((tm, k), lambda i, j: (i, 0)),
            pl.BlockSpec((1, k), lambda i, j: (0, 0)),
            pl.BlockSpec((k, tn), lambda i, j: (0, j)),
        ],
        out_specs=pl.BlockSpec((tm, tn), lambda i, j: (i, j)),
        out_shape=jax.ShapeDtypeStruct((m, n_out), out_dtype),
        scratch_shapes=[pltpu.VMEM((tm, k), BF16)],
        compiler_params=_cparams("parallel", "arbitrary"),
    )(x, nw.reshape(1, k), w)


def _mm_res_kernel(a_ref, w_ref, r_ref, *rest, final_norm):
    if final_norm:
        nw_ref, o_ref = rest
    else:
        (o_ref,) = rest
    kk = pl.program_id(1)

    @pl.when(kk == 0)
    def _():
        o_ref[...] = r_ref[...]

    o_ref[...] += jnp.dot(a_ref[...], _bf(w_ref[...]), preferred_element_type=F32)

    if final_norm:
        @pl.when(kk == pl.num_programs(1) - 1)
        def _():
            o_ref[...] = _rms(o_ref[...], nw_ref[...])


def matmul_res(a, w, res, final_nw=None, tm=1024, tk=512):
    m, k = a.shape
    w, layer = _layered(w)
    n = w.shape[2]
    tm = min(tm, m)
    in_specs = [
        pl.BlockSpec((tm, tk), lambda i, kk: (i, kk)),
        _w_spec((tk, n), lambda i, kk: (kk, 0), layer),
        pl.BlockSpec((tm, n), lambda i, kk: (i, 0)),
    ]
    args = [a, w, res]
    if final_nw is not None:
        in_specs.append(pl.BlockSpec((1, n), lambda i, kk: (0, 0)))
        args.append(final_nw.reshape(1, n))
    return pl.pallas_call(
        functools.partial(_mm_res_kernel, final_norm=final_nw is not None),
        name="mm_res",
        grid=(m // tm, k // tk),
        in_specs=in_specs,
        out_specs=pl.BlockSpec((tm, n), lambda i, kk: (i, 0)),
        out_shape=jax.ShapeDtypeStruct((m, n), F32),
        compiler_params=_cparams("parallel", "arbitrary"),
    )(*args)


def _ffn_up_seq_kernel(x_ref, nw_ref, wg_ref, wu_ref, cw_ref, cb_ref, act_ref, st_ref, h_ref, carry_ref, buf_ref):
    i, j = pl.program_id(0), pl.program_id(1)
    tm = x_ref.shape[0]

    @pl.when(j == 0)
    def _():
        h_ref[...] = _bf(_rms(x_ref[...], nw_ref[...]))

    @pl.when(i == 0)
    def _():
        carry_ref[j] = jnp.zeros(carry_ref.shape[1:], F32)

    h = h_ref[...]
    g = jnp.dot(h, _bf(wg_ref[...]), preferred_element_type=F32)
    u = jnp.dot(h, _bf(wu_ref[...]), preferred_element_type=F32)
    buf_ref[0:8, :] = carry_ref[j]
    buf_ref[8:8 + tm, :] = g
    cw = cw_ref[...]
    a = cw[0:1] * buf_ref[6:6 + tm, :] + cw[1:2] * buf_ref[7:7 + tm, :] + cw[2:3] * g
    act_ref[...] = _bf(_silu(a + cb_ref[...]) * u)
    carry_ref[j] = g[tm - 8:tm]
    st_ref[0] = g[tm - 8:tm]


def ffn_up_seq(x, nw, wg, wu, cw, cb, tm=1024, tn=512):
    m, k = x.shape
    (wg, lg), (wu, lu) = _layered(wg), _layered(wu)
    f = wg.shape[2]
    nb = f // tn
    act, st = pl.pallas_call(
        _ffn_up_seq_kernel,
        name="ffn_up_seq",
        grid=(m // tm, nb),
        in_specs=[
            pl.BlockSpec((tm, k), lambda i, j: (i, 0)),
            pl.BlockSpec((1, k), lambda i, j: (0, 0)),
            _w_spec((k, tn), lambda i, j: (0, j), lg),
            _w_spec((k, tn), lambda i, j: (0, j), lu),
            pl.BlockSpec((FFN_CONV, tn), lambda i, j: (0, j)),
            pl.BlockSpec((1, tn), lambda i, j: (0, j)),
        ],
        out_specs=[
            pl.BlockSpec((tm, tn), lambda i, j: (i, j)),
            pl.BlockSpec((1, 8, tn), lambda i, j: (i, 0, j)),
        ],
        out_shape=[
            jax.ShapeDtypeStruct((m, f), BF16),
            jax.ShapeDtypeStruct((m // tm, 8, f), F32),
        ],
        scratch_shapes=[
            pltpu.VMEM((tm, k), BF16),
            pltpu.VMEM((nb, 8, tn), F32),
            pltpu.VMEM((tm + 8, tn), F32),
        ],
        compiler_params=_cparams("arbitrary", "arbitrary"),
    )(x, nw.reshape(1, k), wg, wu, cw, cb.reshape(1, f))
    return act, st[-1, 8 - (FFN_CONV - 1):]


def _ffn_up_step_kernel(x_ref, nw_ref, wg_ref, wu_ref, cw_ref, cb_ref, p0_ref, p1_ref, act_ref, g_ref, h_ref):
    @pl.when(pl.program_id(1) == 0)
    def _():
        h_ref[...] = _bf(_rms(x_ref[...], nw_ref[...]))

    h = h_ref[...]
    g = jnp.dot(h, _bf(wg_ref[...]), preferred_element_type=F32)
    u = jnp.dot(h, _bf(wu_ref[...]), preferred_element_type=F32)
    cw = cw_ref[...]
    a = cw[0:1] * p0_ref[...] + cw[1:2] * p1_ref[...] + cw[2:3] * g
    act_ref[...] = _bf(_silu(a + cb_ref[...]) * u)
    g_ref[...] = g


def ffn_up_step(x, nw, wg, wu, cw, cb, past, tn=512):
    m, k = x.shape
    (wg, lg), (wu, lu) = _layered(wg), _layered(wu)
    f = wg.shape[2]
    nb = f // tn
    past2 = past.reshape(m, (FFN_CONV - 1) * f)
    act, g = pl.pallas_call(
        _ffn_up_step_kernel,
        name="ffn_up_step",
        grid=(1, nb),
        in_specs=[
            pl.BlockSpec((m, k), lambda i, j: (0, 0)),
            pl.BlockSpec((1, k), lambda i, j: (0, 0)),
            _w_spec((k, tn), lambda i, j: (0, j), lg),
            _w_spec((k, tn), lambda i, j: (0, j), lu),
            pl.BlockSpec((FFN_CONV, tn), lambda i, j: (0, j)),
            pl.BlockSpec((1, tn), lambda i, j: (0, j)),
            pl.BlockSpec((m, tn), lambda i, j: (0, j)),
            pl.BlockSpec((m, tn), lambda i, j: (0, j + nb)),
        ],
        out_specs=[
            pl.BlockSpec((m, tn), lambda i, j: (0, j)),
            pl.BlockSpec((m, tn), lambda i, j: (0, j)),
        ],
        out_shape=[
            jax.ShapeDtypeStruct((m, f), BF16),
            jax.ShapeDtypeStruct((m, f), F32),
        ],
        scratch_shapes=[pltpu.VMEM((m, k), BF16)],
        compiler_params=_cparams("arbitrary", "arbitrary"),
    )(x, nw.reshape(1, k), wg, wu, cw, cb.reshape(1, f), past2, past2)
    return act, jnp.stack([past[:, 1], g], axis=1)


GDN_HPS = 8


def _cumsum_rows(x):
    row = lax.broadcasted_iota(jnp.int32, x.shape, 0)
    s = 1
    while s < x.shape[0]:
        x = x + jnp.where(row >= s, pltpu.roll(x, s, 0), 0.0)
        s *= 2
    return x


def _softplus(x):
    return jnp.maximum(x, 0.0) + jnp.log1p(jnp.exp(-jnp.abs(x)))


def _l2norm(x):
    return x * lax.rsqrt(jnp.sum(x * x, axis=-1, keepdims=True) + 1e-6)


def _gdn_seq_kernel(qp_ref, kp_ref, vp_ref, qh_ref, kh_ref, vh_ref, qpast_ref, kpast_ref, vpast_ref,
                    qw_ref, kw_ref, vw_ref, z_ref, ab_ref, alog_ref, dtb_ref, nw_ref, s0_ref,
                    o_ref, sout_ref, s_scr, buf_ref):
    hg, c = pl.program_id(0), pl.program_id(1)
    C = GDN_CHUNK
    D = HEAD_DIM

    @pl.when(c == 0)
    def _():
        s_scr[...] = s0_ref[...]

    def conv(p_ref, h_ref, past_ref, w_ref):
        buf_ref[0:8, :] = jnp.where(c == 0, past_ref[...], h_ref[...])
        buf_ref[8:8 + C, :] = p_ref[...]
        w = w_ref[...]
        y = (w[0:1] * buf_ref[5:5 + C, :] + w[1:2] * buf_ref[6:6 + C, :]
             + w[2:3] * buf_ref[7:7 + C, :] + w[3:4] * buf_ref[8:8 + C, :])
        return _silu(y)

    qc = conv(qp_ref, qh_ref, qpast_ref, qw_ref)
    kc = conv(kp_ref, kh_ref, kpast_ref, kw_ref)
    vc = conv(vp_ref, vh_ref, vpast_ref, vw_ref)

    ab = ab_ref[...]
    lane = lax.broadcasted_iota(jnp.int32, (C, 128), 1)
    g_all = -jnp.exp(alog_ref[...]) * _softplus(ab + dtb_ref[...])
    gc_all = _cumsum_rows(g_all)
    beta_all = _sigmoid(ab)

    row = lax.broadcasted_iota(jnp.int32, (C, C), 0)
    col = lax.broadcasted_iota(jnp.int32, (C, C), 1)
    eye = (row == col).astype(F32)
    nw = nw_ref[...]

    hs = range(GDN_HPS)
    sl = [slice(j * D, (j + 1) * D) for j in hs]
    gcol = [jnp.sum(jnp.where(lane == hg * GDN_HPS + j, gc_all, 0.0), axis=1, keepdims=True) for j in hs]
    bcol = [jnp.sum(jnp.where(lane == hg * GDN_HPS + j + GDN_HEADS, beta_all, 0.0), axis=1, keepdims=True)
            for j in hs]
    colb = [jnp.broadcast_to(gcol[j], (C, C)) for j in hs]
    dec = [jnp.exp(jnp.minimum(colb[j] - colb[j].T, 0.0)) for j in hs]
    q = [_l2norm(qc[:, sl[j]]) * (D ** -0.5) for j in hs]
    k = [_l2norm(kc[:, sl[j]]) for j in hs]
    kb = [k[j] * bcol[j] for j in hs]
    a_dec = [_dot_nt(kb[j], k[j]) * dec[j] for j in hs]
    qk = [jnp.where(row >= col, _dot_nt(q[j], k[j]) * dec[j], 0.0) for j in hs]
    first = ((row ^ col) == 1) & ((row & 1) == 1)
    tm = [eye - jnp.where(first, a_dec[j], 0.0) for j in hs]
    s = 2
    while s < C:
        msk = ((row // (2 * s)) == (col // (2 * s))) & ((row & s) != 0) & ((col & s) == 0)
        x = [_dot(jnp.where(msk, a_dec[j], 0.0), tm[j]) for j in hs]
        tm = [tm[j] - _dot(tm[j], x[j]) for j in hs]
        s *= 2
    egc = [jnp.exp(gcol[j]) for j in hs]
    uw = [_dot(tm[j], jnp.concatenate([vc[:, sl[j]] * bcol[j], kb[j] * egc[j]], axis=1)) for j in hs]
    r = [_dot(jnp.concatenate([uw[j][:, D:], q[j] * egc[j]], axis=0), s_scr[j]) for j in hs]
    v_new = [uw[j][:, :D] - r[j][:C] for j in hs]
    o = [r[j][C:] + _dot(qk[j], v_new[j]) for j in hs]
    for j in hs:
        g_last = colb[j][C - 1:C, :]
        kdec = k[j] * jnp.exp(g_last[:, 0:1] - gcol[j])
        s_scr[j] = s_scr[j] * jnp.exp(g_last) + _dot_tn(kdec, v_new[j])
        o_ref[:, sl[j]] = _bf(_rms(o[j], nw) * _silu(z_ref[:, sl[j]]))

    @pl.when(c == pl.num_programs(1) - 1)
    def _():
        sout_ref[...] = s_scr[...]


def gdn_seq(proj, ab, conv_w, a_log, dt_bias, norm_w, past8, s0):
    t = proj.shape[0]
    C, W = GDN_CHUNK, GDN_HPS * HEAD_DIM
    nq = GDN_HEADS * HEAD_DIM // W
    pad16 = lambda a: jnp.pad(a.reshape(1, GDN_HEADS), ((0, 0), (0, 128 - GDN_HEADS)))

    def rows(off):
        return pl.BlockSpec((C, W), lambda hg, c: (c, off + hg))

    def halo(off):
        return pl.BlockSpec((8, W), lambda hg, c: (jnp.maximum(c * (C // 8) - 1, 0), off + hg))

    def fixed(rws, off):
        return pl.BlockSpec((rws, W), lambda hg, c: (0, off + hg))

    o, s_out = pl.pallas_call(
        _gdn_seq_kernel,
        name="gdn_seq",
        grid=(nq, t // C),
        in_specs=[rows(0), rows(nq), rows(2 * nq), halo(0), halo(nq), halo(2 * nq),
                  fixed(8, 0), fixed(8, nq), fixed(8, 2 * nq),
                  fixed(GDN_CONV, 0), fixed(GDN_CONV, nq), fixed(GDN_CONV, 2 * nq),
                  rows(3 * nq),
                  pl.BlockSpec((C, 128), lambda hg, c: (c, 0)),
                  pl.BlockSpec((1, 128), lambda hg, c: (0, 0)),
                  pl.BlockSpec((1, 128), lambda hg, c: (0, 0)),
                  pl.BlockSpec((1, HEAD_DIM), lambda hg, c: (0, 0)),
                  pl.BlockSpec((GDN_HPS, HEAD_DIM, HEAD_DIM), lambda hg, c: (hg, 0, 0))],
        out_specs=[pl.BlockSpec((C, W), lambda hg, c: (c, hg)),
                   pl.BlockSpec((GDN_HPS, HEAD_DIM, HEAD_DIM), lambda hg, c: (hg, 0, 0))],
        out_shape=[jax.ShapeDtypeStruct((t, GDN_HEADS * HEAD_DIM), BF16),
                   jax.ShapeDtypeStruct((GDN_HEADS, HEAD_DIM, HEAD_DIM), F32)],
        scratch_shapes=[pltpu.VMEM((GDN_HPS, HEAD_DIM, HEAD_DIM), F32),
                        pltpu.VMEM((C + 8, W), F32)],
        compiler_params=_cparams("parallel", "arbitrary"),
    )(proj, proj, proj, proj, proj, proj, past8, past8, past8,
      conv_w, conv_w, conv_w, proj, ab, pad16(a_log), pad16(dt_bias), norm_w.reshape(1, HEAD_DIM), s0)
    return o, s_out


def _pad_cols(w, n):
    return jnp.pad(w, ((0, 0), (0, n - w.shape[1])))


def gdn_layer_seq(x, nmix, w_in, conv_w, a_log, dt_bias, norm_w, w_out):
    nqkv = 3 * GDN_HEADS * HEAD_DIM
    nz = GDN_HEADS * HEAD_DIM
    proj = rms_matmul(x, nmix, w_in, nqkv + nz, 512)
    ab = rms_matmul(x, nmix, _pad_cols(w_in[:, nqkv + nz:], 128), 128, 128)
    past8 = jnp.zeros((8, nqkv), F32)
    s0 = jnp.zeros((GDN_HEADS, HEAD_DIM, HEAD_DIM), F32)
    o, s_out = gdn_seq(proj, ab, conv_w, a_log, dt_bias, norm_w, past8, s0)
    x1 = matmul_res(o, w_out, x)
    return x1, s_out, proj[-(GDN_CONV - 1):, :nqkv]


def ffn_layer_seq(x, nw, wg, wu, cw, cb, wd, final_nw=None):
    act, st = ffn_up_seq(x, nw, wg, wu, cw, cb)
    return matmul_res(act, wd, x, final_nw=final_nw), st


def ffn_layer_step(x, nw, wg, wu, cw, cb, wd, past, final_nw=None):
    act, st = ffn_up_step(x, nw, wg, wu, cw, cb, past)
    return matmul_res(act, wd, x, final_nw=final_nw), st


KV_SECTION = NSA_GROUPS * HEAD_DIM
KV_ROTARY_SECTIONS = (2, 4)


def _rope_tables(pos):
    half = HEAD_DIM // 2
    inv = jnp.float32(ROPE_THETA) ** (-jnp.arange(half, dtype=F32) / half)
    ang = pos.astype(F32)[:, None] * inv[None, :]
    cos, sin = jnp.cos(ang), jnp.sin(ang)
    return jnp.concatenate([cos, cos], axis=1), jnp.concatenate([-sin, sin], axis=1)


def _rope(x, cosf, sinf):
    return x * cosf + pltpu.roll(x, HEAD_DIM // 2, 1) * sinf


def _kv_rows_kernel(x_ref, nw_ref, w_ref, cos_ref, sin_ref, o_ref, ob_ref, h_ref):
    j = pl.program_id(1)

    @pl.when(j == 0)
    def _():
        h_ref[...] = _bf(_rms(x_ref[...], nw_ref[...]))

    y = jnp.dot(h_ref[...], _bf(w_ref[...]), preferred_element_type=F32)
    is_rot = (j == KV_ROTARY_SECTIONS[0]) | (j == KV_ROTARY_SECTIONS[1])

    @pl.when(is_rot)
    def _():
        cosf, sinf = cos_ref[...], sin_ref[...]
        for g in range(NSA_GROUPS):
            sl = slice(g * HEAD_DIM, (g + 1) * HEAD_DIM)
            yg = _rope(y[:, sl], cosf, sinf)
            o_ref[:, sl] = yg
            ob_ref[:, sl] = _bf(yg)

    @pl.when(jnp.logical_not(is_rot))
    def _():
        o_ref[...] = y
        ob_ref[...] = _bf(y)


def kv_rows(x, nw, w, cosf, sinf, tm=1024):
    m, k = x.shape
    n = w.shape[1]
    tm = min(tm, m)
    return pl.pallas_call(
        _kv_rows_kernel,
        name="kv_rows",
        grid=(m // tm, n // KV_SECTION),
        in_specs=[
            pl.BlockSpec((tm, k), lambda i, j: (i, 0)),
            pl.BlockSpec((1, k), lambda i, j: (0, 0)),
            pl.BlockSpec((k, KV_SECTION), lambda i, j: (0, j)),
            pl.BlockSpec((tm, HEAD_DIM), lambda i, j: (i, 0)),
            pl.BlockSpec((tm, HEAD_DIM), lambda i, j: (i, 0)),
        ],
        out_specs=[pl.BlockSpec((tm, KV_SECTION), lambda i, j: (i, j)),
                   pl.BlockSpec((tm, KV_SECTION), lambda i, j: (i, j))],
        out_shape=[jax.ShapeDtypeStruct((m, n), F32), jax.ShapeDtypeStruct((m, n), BF16)],
        scratch_shapes=[pltpu.VMEM((tm, k), BF16)],
        compiler_params=_cparams("parallel", "arbitrary"),
    )(x, nw.reshape(1, k), w, cosf, sinf)


def _compress_seq_kernel(rows_ref, pe_ref, w1_ref, w2_ref, o_ref, xs_ref, ps_ref):
    nseg = o_ref.shape[2]
    half = CMP_STRIDE * HEAD_DIM
    ps_ref[...] = jnp.zeros(ps_ref.shape, F32)
    for t in range(CMP_STRIDE):
        sl = slice(t * HEAD_DIM, (t + 1) * HEAD_DIM)
        xs_ref[:, sl] = _bf(rows_ref[pl.ds(t, nseg, stride=CMP_STRIDE), :])
        ps_ref[0:1, sl] = pe_ref[0, t:t + 1, :]
        ps_ref[8:9, sl] = pe_ref[0, CMP_STRIDE + t:CMP_STRIDE + t + 1, :]
    xs = xs_ref[...]
    w_lo = _bf(w1_ref[0, 0:half, :])
    w_hi = _bf(w1_ref[0, half:2 * half, :])
    lo = jnp.dot(xs, w_lo, preferred_element_type=F32) + _dot(ps_ref[0:8, :], w_lo)[0:1]
    hi = jnp.dot(xs, w_hi, preferred_element_type=F32) + _dot(ps_ref[8:16, :], w_hi)[0:1]
    hid = _silu(lo + pltpu.roll(hi, nseg - 1, 0))
    o_ref[0, 0] = _bf(_dot(hid, w2_ref[0]))


def compress_seq(kv, pe, w1, w2):
    t = kv.shape[0]
    nseg = t // CMP_STRIDE
    hid = w1.shape[2]
    return pl.pallas_call(
        _compress_seq_kernel,
        name="compress_seq",
        grid=(2, NSA_GROUPS),
        in_specs=[
            pl.BlockSpec((t, HEAD_DIM), lambda a, g: (0, a * NSA_GROUPS + g)),
            pl.BlockSpec((1, CMP_BLOCK, HEAD_DIM), lambda a, g: (a, 0, 0)),
            pl.BlockSpec((1, CMP_BLOCK * HEAD_DIM, hid), lambda a, g: (a, 0, 0)),
            pl.BlockSpec((1, hid, HEAD_DIM), lambda a, g: (a, 0, 0)),
        ],
        out_specs=pl.BlockSpec((1, 1, nseg, HEAD_DIM), lambda a, g: (a, g, 0, 0)),
        out_shape=jax.ShapeDtypeStruct((2, NSA_GROUPS, nseg, HEAD_DIM), BF16),
        scratch_shapes=[pltpu.VMEM((nseg, CMP_STRIDE * HEAD_DIM), BF16),
                        pltpu.VMEM((16, CMP_STRIDE * HEAD_DIM), F32)],
        compiler_params=_cparams("arbitrary", "arbitrary"),
    )(kv, pe, w1, w2)


def _group_rows(ref, g):
    return jnp.concatenate([ref[:, (NSA_HPG * g + h) * HEAD_DIM:(NSA_HPG * g + h + 1) * HEAD_DIM]
                            for h in range(NSA_HPG)], axis=0)


def _masked_softmax(s, mask, exp=jnp.exp):
    s = jnp.where(mask, s, NEG_BIG)
    m = jnp.max(s, axis=1, keepdims=True)
    e = jnp.where(mask, exp(s - m), 0.0)
    return e / jnp.maximum(jnp.sum(e, axis=1, keepdims=True), 1e-30)


def _split3(x):
    hi = _bf(x)
    r = x - hi.astype(F32)
    mid = _bf(r)
    return hi, mid, _bf(r - mid.astype(F32))


def _select_blocks(score, blk, n_pick):
    sel = jnp.zeros(score.shape, F32)
    for _ in range(n_pick):
        mx = jnp.max(score, axis=1, keepdims=True)
        idx = jnp.min(jnp.where(score == mx, blk, 1e9), axis=1, keepdims=True)
        pick = blk == idx
        sel = jnp.where(pick, 1.0, sel)
        score = jnp.where(pick, -3.0, score)
    return sel


def _select_blocks_ranked(score, n_rows, n_pick):
    n = score.shape[1]
    st = score.T
    ii = lax.broadcasted_iota(jnp.int32, (n, n), 0)
    jj = lax.broadcasted_iota(jnp.int32, (n, n), 1)
    rows = []
    for r in range(n_rows):
        mine = jnp.broadcast_to(score[r:r + 1, :], (n, n))
        other = jnp.broadcast_to(st[:, r:r + 1], (n, n))
        beats = jnp.where(other > mine, 1.0, jnp.where((other == mine) & (ii < jj), 1.0, 0.0))
        rank = jnp.sum(beats, axis=0, keepdims=True)
        rows.append(jnp.where(rank < n_pick, 1.0, 0.0))
    return _pad_rows(rows, score.shape[0])


def _nsa_cmp_kernel(q_ref, cos_ref, sin_ref, kc_ref, vc_ref, oc_ref, sel_ref, qrot_ref, *, n_sel):
    i = pl.program_id(0)
    Q = Q_BLOCK
    nc = kc_ref.shape[1]
    scale = HEAD_DIM ** -0.5
    cosf, sinf = cos_ref[...], sin_ref[...]
    for h in range(NSA_HEADS):
        sl = slice(h * HEAD_DIM, (h + 1) * HEAD_DIM)
        qrot_ref[:, sl] = _bf(_rope(q_ref[:, sl], cosf, sinf) * (scale * LOG2E))

    qpos_r = i * Q + lax.broadcasted_iota(jnp.int32, (NSA_HPG * Q, nc), 0) % Q
    c_end = lax.broadcasted_iota(jnp.int32, (NSA_HPG * Q, nc), 1) * CMP_STRIDE + (CMP_BLOCK - 1)
    cmask = c_end <= qpos_r
    ci = lax.broadcasted_iota(jnp.int32, (nc, 128), 0) * CMP_STRIDE
    sj = lax.broadcasted_iota(jnp.int32, (nc, 128), 1) * SEL_BLOCK
    cover = jnp.where((ci < sj + SEL_BLOCK) & (ci + CMP_BLOCK > sj), 1.0, 0.0).astype(BF16)
    qpos = i * Q + lax.broadcasted_iota(jnp.int32, (Q, 128), 0)
    blk = lax.broadcasted_iota(jnp.int32, (Q, 128), 1)
    valid = blk * SEL_BLOCK <= qpos
    force = (blk == 0) | (blk >= qpos // SEL_BLOCK - 1)
    blkf = blk.astype(F32)

    scores = []
    for g in range(NSA_GROUPS):
        p = _masked_softmax(_dot_nt(_group_rows(q_ref, g), kc_ref[g]) * scale, cmask)
        oc = _dot(p, vc_ref[g])
        for h in range(NSA_HPG):
            oc_ref[:, (NSA_HPG * g + h) * HEAD_DIM:(NSA_HPG * g + h + 1) * HEAD_DIM] = oc[h * Q:(h + 1) * Q]
        ps = p[0:Q] + p[Q:2 * Q] + p[2 * Q:3 * Q] + p[3 * Q:4 * Q]
        imp = sum(jnp.dot(part, cover, preferred_element_type=F32) for part in _split3(ps))
        score = jnp.where(valid, jnp.where(force, 1e9, imp), -1.0)
        scores.append(jnp.where(blk < n_sel, score, -2.0))
    picked = _select_blocks(jnp.concatenate(scores, axis=0), jnp.concatenate([blkf] * NSA_GROUPS, axis=0), N_SELECT)
    for g in range(NSA_GROUPS):
        sel_ref[:, g * 128:(g + 1) * 128] = _bf(jnp.where(picked[g * Q:(g + 1) * Q] > 0.5, 0.0, NEG_BIG))


def nsa_cmp(q, cosf, sinf, kc, vc):
    t, d = q.shape
    n_sel = t // SEL_BLOCK
    assert N_SELECT <= n_sel <= 128 and t % Q_BLOCK == 0
    nc = kc.shape[1]
    row = lambda w: pl.BlockSpec((Q_BLOCK, w), lambda i: (i, 0))
    full = pl.BlockSpec((NSA_GROUPS, nc, HEAD_DIM), lambda i: (0, 0, 0))
    return pl.pallas_call(
        functools.partial(_nsa_cmp_kernel, n_sel=n_sel),
        name="nsa_cmp",
        grid=(t // Q_BLOCK,),
        in_specs=[row(d), row(HEAD_DIM), row(HEAD_DIM), full, full],
        out_specs=[row(d), row(NSA_GROUPS * 128), row(d)],
        out_shape=[jax.ShapeDtypeStruct((t, d), F32), jax.ShapeDtypeStruct((t, NSA_GROUPS * 128), BF16),
                   jax.ShapeDtypeStruct((t, d), BF16)],
        compiler_params=_cparams("parallel"),
    )(q, cosf, sinf, kc, vc)


SEL_KEYS = 512


def _nsa_sel_kernel(qi_ref, kt_ref, q_ref, sel_ref, k_ref, v_ref, o_ref, m_scr, l_scr, acc_scr):
    n = pl.program_id(0)
    i, kt = qi_ref[n], kt_ref[n]
    Q = Q_BLOCK
    last = (i * Q + Q - 1) // SEL_KEYS

    @pl.when(kt == 0)
    def _():
        m_scr[...] = jnp.full(m_scr.shape, NEG_BIG, F32)
        l_scr[...] = jnp.zeros(l_scr.shape, F32)
        acc_scr[...] = jnp.zeros(acc_scr.shape, F32)

    def tile(diagonal):
        brow = lax.broadcasted_iota(jnp.int32, (128, SEL_KEYS), 0)
        kcol = lax.broadcasted_iota(jnp.int32, (128, SEL_KEYS), 1)
        expand = jnp.where(brow == kt * (SEL_KEYS // SEL_BLOCK) + kcol // SEL_BLOCK, 1.0, 0.0).astype(BF16)
        if diagonal:
            qpos = i * Q + lax.broadcasted_iota(jnp.int32, (Q, SEL_KEYS), 0)
            kpos = kt * SEL_KEYS + lax.broadcasted_iota(jnp.int32, (Q, SEL_KEYS), 1)
            causal = kpos <= qpos
        gs = range(NSA_GROUPS)
        sl = [slice(g * HEAD_DIM, (g + 1) * HEAD_DIM) for g in gs]
        bias = [jnp.dot(sel_ref[:, g * 128:(g + 1) * 128], expand, preferred_element_type=F32) for g in gs]
        if diagonal:
            bias = [jnp.where(causal, b, NEG_BIG) for b in bias]
        s = [_dot_nt(_group_rows(q_ref, g), k_ref[:, sl[g]]) for g in gs]
        s = [(s[g].reshape(NSA_HPG, Q, SEL_KEYS) + bias[g][None]).reshape(NSA_HPG * Q, SEL_KEYS) for g in gs]
        cols = [slice(c * 128, (c + 1) * 128) for c in range(SEL_KEYS // 128)]
        rep = lambda x: jnp.broadcast_to(x, (NSA_HPG * Q, 128))
        m_old = [m_scr[g] for g in gs]
        m_new = [jnp.maximum(m_old[g], rep(jnp.max(s[g], axis=1, keepdims=True))) for g in gs]
        e = [jnp.concatenate([jnp.exp2(s[g][:, c] - m_new[g]) for c in cols], axis=1) for g in gs]
        pv = [_dot(e[g], v_ref[:, sl[g]]) for g in gs]
        for g in gs:
            alpha = jnp.exp2(m_old[g] - m_new[g])
            l_scr[g] = alpha * l_scr[g] + rep(jnp.sum(e[g], axis=1, keepdims=True))
            acc_scr[g] = alpha * acc_scr[g] + pv[g]
            m_scr[g] = m_new[g]

    pl.when(kt < last)(functools.partial(tile, False))
    pl.when(kt == last)(functools.partial(tile, True))

    @pl.when(kt == last)
    def _():
        for g in range(NSA_GROUPS):
            o = acc_scr[g] / jnp.maximum(l_scr[g], 1e-30)
            for h in range(NSA_HPG):
                o_ref[:, (NSA_HPG * g + h) * HEAD_DIM:(NSA_HPG * g + h + 1) * HEAD_DIM] = o[h * Q:(h + 1) * Q]


def nsa_sel(qrot, sel, kvb):
    t, d = qrot.shape
    pairs = [(i, kt) for i in range(t // Q_BLOCK) for kt in range((i * Q_BLOCK + Q_BLOCK - 1) // SEL_KEYS + 1)]
    qi = jnp.asarray(np.array([p[0] for p in pairs], np.int32))
    kti = jnp.asarray(np.array([p[1] for p in pairs], np.int32))
    qblk = lambda w: pl.BlockSpec((Q_BLOCK, w), lambda n, qi, kti: (qi[n], 0))
    kblk = lambda sec: pl.BlockSpec((SEL_KEYS, KV_SECTION), lambda n, qi, kti: (kti[n], sec))
    return pl.pallas_call(
        _nsa_sel_kernel,
        name="nsa_sel",
        grid_spec=pltpu.PrefetchScalarGridSpec(
            num_scalar_prefetch=2,
            grid=(len(pairs),),
            in_specs=[qblk(d), qblk(NSA_GROUPS * 128), kblk(2), kblk(3)],
            out_specs=qblk(d),
            scratch_shapes=[pltpu.VMEM((NSA_GROUPS, NSA_HPG * Q_BLOCK, 128), F32),
                            pltpu.VMEM((NSA_GROUPS, NSA_HPG * Q_BLOCK, 128), F32),
                            pltpu.VMEM((NSA_GROUPS, NSA_HPG * Q_BLOCK, HEAD_DIM), F32)],
        ),
        out_shape=jax.ShapeDtypeStruct((t, d), F32),
        compiler_params=_cparams("arbitrary"),
    )(qi, kti, qrot, sel, kvb, kvb)


WIN_BLOCKS = WINDOW // Q_BLOCK + 1


def _nsa_win_kernel(q_ref, *refs):
    k_refs, v_refs = refs[:WIN_BLOCKS], refs[WIN_BLOCKS:2 * WIN_BLOCKS]
    oc_ref, os_ref, gate_ref, o_ref = refs[2 * WIN_BLOCKS:]
    i = pl.program_id(0)
    Q = Q_BLOCK
    span = WIN_BLOCKS * Q
    kw = jnp.concatenate([r[...] for r in k_refs], axis=0)
    vw = jnp.concatenate([r[...] for r in v_refs], axis=0)
    qpos = i * Q + lax.broadcasted_iota(jnp.int32, (NSA_HPG * Q, span), 0) % Q
    kpos = (i - (WIN_BLOCKS - 1)) * Q + lax.broadcasted_iota(jnp.int32, (NSA_HPG * Q, span), 1)
    mask = (kpos <= qpos) & (kpos > qpos - WINDOW) & (kpos >= 0)
    gates = _sigmoid(gate_ref[...])
    for g in range(NSA_GROUPS):
        sl = slice(g * HEAD_DIM, (g + 1) * HEAD_DIM)
        p = _masked_softmax(_dot_nt(_group_rows(q_ref, g), kw[:, sl]), mask, exp=jnp.exp2)
        ow = _dot(p, vw[:, sl])
        for h in range(NSA_HPG):
            head = NSA_HPG * g + h
            hs = slice(head * HEAD_DIM, (head + 1) * HEAD_DIM)
            o = (gates[:, 3 * head:3 * head + 1] * oc_ref[:, hs] + gates[:, 3 * head + 1:3 * head + 2] * os_ref[:, hs]
                 + gates[:, 3 * head + 2:3 * head + 3] * ow[h * Q:(h + 1) * Q])
            o_ref[:, hs] = _bf(o)


def nsa_win(qrot, kvb, oc, osel, gates):
    t, d = qrot.shape
    row = lambda w: pl.BlockSpec((Q_BLOCK, w), lambda i: (i, 0))
    kblk = lambda sec, b: pl.BlockSpec(
        (Q_BLOCK, KV_SECTION), lambda i: (jnp.maximum(i - (WIN_BLOCKS - 1) + b, 0), sec))
    return pl.pallas_call(
        _nsa_win_kernel,
        name="nsa_win",
        grid=(t // Q_BLOCK,),
        in_specs=([row(d)] + [kblk(4, b) for b in range(WIN_BLOCKS)] + [kblk(5, b) for b in range(WIN_BLOCKS)]
                  + [row(d), row(d), row(128)]),
        out_specs=row(d),
        out_shape=jax.ShapeDtypeStruct((t, d), BF16),
        compiler_params=_cparams("parallel"),
    )(qrot, *([kvb] * (2 * WIN_BLOCKS)), oc, osel, gates)


def nsa_layer_seq(x, nmix, w_q, w_out, kvb, kc, vc, cosf, sinf):
    nq = NSA_HEADS * HEAD_DIM
    q = rms_matmul(x, nmix, w_q, nq, 512)
    gates = rms_matmul(x, nmix, _pad_cols(w_q[:, nq:], 128), 128, 128)
    oc, sel, qrot = nsa_cmp(q, cosf, sinf, kc, vc)
    osel = nsa_sel(qrot, sel, kvb)
    o = nsa_win(qrot, kvb, oc, osel, gates)
    return matmul_res(o, w_out, x)


def _head_column(row, offset, n_heads):
    h = lax.broadcasted_iota(jnp.int32, (n_heads, 128), 0)
    lane = lax.broadcasted_iota(jnp.int32, (n_heads, 128), 1)
    return jnp.sum(jnp.where(lane == h + offset, row, 0.0), axis=1, keepdims=True)


def _pad_rows(rows, n):
    return jnp.concatenate(rows + [jnp.zeros((n - len(rows), rows[0].shape[1]), F32)], axis=0)


def _gdn_step_kernel(proj_ref, ab_ref, past_ref, cw_ref, alog_ref, dtb_ref, nw_ref, s_ref, o_ref, sout_ref):
    H, D = GDN_HEADS, HEAD_DIM
    proj = proj_ref[0]
    cw = cw_ref[...]
    y = cw[GDN_CONV - 1] * proj[0:3 * H]
    for j in range(GDN_CONV - 1):
        y = y + cw[j] * past_ref[0, j]
    y = _silu(y)
    q = _l2norm(y[0:H]) * (D ** -0.5)
    k = _l2norm(y[H:2 * H])
    v = y[2 * H:3 * H]
    ab = ab_ref[0]
    g_row = -jnp.exp(alog_ref[...]) * _softplus(ab + dtb_ref[...])
    eg = jnp.exp(_head_column(g_row, 0, H))
    beta = _head_column(_sigmoid(ab), H, H)
    kb = k * beta
    w = kb * eg
    qe = q * eg
    rs = [_dot(_pad_rows([w[h:h + 1], qe[h:h + 1]], 8), s_ref[0, h]) for h in range(H)]
    ws = jnp.concatenate([r[0:1] for r in rs], axis=0)
    qs = jnp.concatenate([r[1:2] for r in rs], axis=0)
    v_new = v * beta - ws
    o = qs + jnp.sum(q * k, axis=1, keepdims=True) * v_new
    for h in range(H):
        outer = _dot_tn(_pad_rows([k[h:h + 1]], 8), _pad_rows([v_new[h:h + 1]], 8))
        sout_ref[0, h] = s_ref[0, h] * eg[h:h + 1] + outer
    o_ref[0] = _bf(_rms(o, nw_ref[...]) * _silu(proj[3 * H:4 * H]))


def gdn_step(proj, ab, conv_past, conv_w, a_log, dt_bias, norm_w, s0):
    b = proj.shape[0]
    H, D = GDN_HEADS, HEAD_DIM
    pad16 = lambda a: jnp.pad(a.reshape(1, H), ((0, 0), (0, 128 - H)))
    o, s_out = pl.pallas_call(
        _gdn_step_kernel,
        name="gdn_step",
        grid=(b,),
        in_specs=[pl.BlockSpec((1, 4 * H, D), lambda i: (i, 0, 0)),
                  pl.BlockSpec((1, 1, 128), lambda i: (i, 0, 0)),
                  pl.BlockSpec((1, GDN_CONV - 1, 3 * H, D), lambda i: (i, 0, 0, 0)),
                  pl.BlockSpec((GDN_CONV, 3 * H, D), lambda i: (0, 0, 0)),
                  pl.BlockSpec((1, 128), lambda i: (0, 0)),
                  pl.BlockSpec((1, 128), lambda i: (0, 0)),
                  pl.BlockSpec((1, D), lambda i: (0, 0)),
                  pl.BlockSpec((1, H, D, D), lambda i: (i, 0, 0, 0))],
        out_specs=[pl.BlockSpec((1, H, D), lambda i: (i, 0, 0)),
                   pl.BlockSpec((1, H, D, D), lambda i: (i, 0, 0, 0))],
        out_shape=[jax.ShapeDtypeStruct((b, H, D), BF16), jax.ShapeDtypeStruct((b, H, D, D), F32)],
        compiler_params=_cparams("parallel"),
    )(proj.reshape(b, 4 * H, D), ab.reshape(b, 1, 128), conv_past.reshape(b, GDN_CONV - 1, 3 * H, D),
      conv_w.reshape(GDN_CONV, 3 * H, D), pad16(a_log), pad16(dt_bias), norm_w.reshape(1, D), s0)
    return o.reshape(b, H * D), s_out


def gdn_layer_step(x, nmix, w_in, conv_w, a_log, dt_bias, norm_w, w_out, conv_past, s0):
    nqkv = 3 * GDN_HEADS * HEAD_DIM
    nz = GDN_HEADS * HEAD_DIM
    proj = rms_matmul(x, nmix, w_in, nqkv + nz, 512)
    ab = rms_matmul(x, nmix, _pad_cols(w_in[:, nqkv + nz:], 128), 128, 128)
    o, s_out = gdn_step(proj, ab, conv_past, conv_w, a_log, dt_bias, norm_w, s0)
    conv_new = jnp.concatenate([conv_past[:, 1:], proj[:, None, :nqkv]], axis=1)
    return matmul_res(o, w_out, x), s_out, conv_new


def _compress_pages_kernel(pt_ref, *refs, n_pages):
    page_refs = refs[:n_pages]
    pe_ref, w1_ref, w2_ref, o_ref, xs_ref, ps_ref = refs[n_pages:]
    G, D = NSA_GROUPS, HEAD_DIM
    seg_pp = page_refs[0].shape[0]
    nseg = n_pages * seg_pp
    half = CMP_STRIDE * D
    ps_ref[...] = jnp.zeros(ps_ref.shape, F32)
    for a in range(2):
        for t in range(CMP_STRIDE):
            sl = slice(t * D, (t + 1) * D)
            ps_ref[a, 0:1, sl] = pe_ref[a, t:t + 1, :]
            ps_ref[a, 8:9, sl] = pe_ref[a, CMP_STRIDE + t:CMP_STRIDE + t + 1, :]
    for p in range(n_pages):
        for t in range(CMP_STRIDE):
            by_vec = jnp.swapaxes(page_refs[p][:, t], 0, 1)
            for a in range(2):
                for g in range(G):
                    xs_ref[a, g * nseg + p * seg_pp:g * nseg + (p + 1) * seg_pp, t * D:(t + 1) * D] = by_vec[a * G + g]
    for a in range(2):
        xs = _bf(xs_ref[a])
        w_lo = w1_ref[a, 0:half, :]
        w_hi = w1_ref[a, half:2 * half, :]
        lo = jnp.dot(xs, w_lo, preferred_element_type=F32) + _dot(ps_ref[a, 0:8, :], w_lo)[0:1]
        hi = jnp.dot(xs, w_hi, preferred_element_type=F32) + _dot(ps_ref[a, 8:16, :], w_hi)[0:1]
        hi = jnp.concatenate([pltpu.roll(hi[g * nseg:(g + 1) * nseg], nseg - 1, 0) for g in range(G)], axis=0)
        o_ref[0, a] = _bf(_dot(_silu(lo + hi), w2_ref[a]))


def compress_pages(cache, page_table, pe, w1b, w2):
    b, n_pages = page_table.shape
    page = cache.shape[1]
    seg_pp = page // CMP_STRIDE
    nseg = n_pages * seg_pp
    hid = w1b.shape[2]
    cache = cache.reshape(cache.shape[0] * seg_pp, CMP_STRIDE, 2 * NSA_GROUPS, HEAD_DIM)

    def page_spec(p):
        return pl.BlockSpec((seg_pp, CMP_STRIDE, 2 * NSA_GROUPS, HEAD_DIM), lambda i, pt: (pt[i, p], 0, 0, 0))

    const = lambda shape: pl.BlockSpec(shape, lambda i, pt: (0,) * len(shape))
    return pl.pallas_call(
        functools.partial(_compress_pages_kernel, n_pages=n_pages),
        name="compress_pages",
        grid_spec=pltpu.PrefetchScalarGridSpec(
            num_scalar_prefetch=1,
            grid=(b,),
            in_specs=[page_spec(p) for p in range(n_pages)] + [
                const((2, CMP_BLOCK, HEAD_DIM)), const((2, CMP_BLOCK * HEAD_DIM, hid)), const((2, hid, HEAD_DIM))],
            out_specs=pl.BlockSpec((1, 2, NSA_GROUPS * nseg, HEAD_DIM), lambda i, pt: (i, 0, 0, 0)),
            scratch_shapes=[pltpu.VMEM((2, NSA_GROUPS * nseg, CMP_STRIDE * HEAD_DIM), F32),
                            pltpu.VMEM((2, 16, CMP_STRIDE * HEAD_DIM), F32)],
        ),
        out_shape=jax.ShapeDtypeStruct((b, 2, NSA_GROUPS * nseg, HEAD_DIM), BF16),
        compiler_params=_cparams("parallel"),
    )(page_table, *([cache] * n_pages), pe, w1b, w2)


def _head_rows_of_group(x, g):
    h = lax.broadcasted_iota(jnp.int32, x.shape, 0)
    return jnp.where(h // NSA_HPG == g, x, 0.0)


def _nsa_step_kernel(pt_ref, *refs, n_pages):
    page_refs = refs[:n_pages]
    q_ref, gate_ref, cos_ref, sin_ref, kvc_ref, new_ref, win_ref, o_ref, nwin_ref = refs[n_pages:]
    H, G, D = NSA_HEADS, NSA_GROUPS, HEAD_DIM
    page = page_refs[0].shape[0] // (2 * G)
    past = n_pages * page
    nseg = past // CMP_STRIDE
    n_sel = past // SEL_BLOCK + 1
    wb = win_ref.shape[0] // (2 * G)
    scale = D ** -0.5
    q_raw = q_ref[0]
    q_rot = _rope(q_raw, cos_ref[...], sin_ref[...])
    nwin_ref[0:(wb - 1) * 2 * G, :] = win_ref[2 * G:wb * 2 * G, :]
    nwin_ref[(wb - 1) * 2 * G:wb * 2 * G, :] = new_ref[4 * G:6 * G, :]

    def token_rows(ref, n, kv, g):
        return ref[pl.ds(kv * G + g, n, stride=2 * G), :]

    lane = lax.broadcasted_iota(jnp.int32, (8, 128), 1)
    c_ok = lax.broadcasted_iota(jnp.int32, (H, nseg), 1) * CMP_STRIDE + (CMP_BLOCK - 1) <= past
    ci = lax.broadcasted_iota(jnp.int32, (nseg, 128), 0) * CMP_STRIDE
    sj = lax.broadcasted_iota(jnp.int32, (nseg, 128), 1) * SEL_BLOCK
    cover = jnp.where((ci < sj + SEL_BLOCK) & (ci + CMP_BLOCK > sj), 1.0, 0.0).astype(BF16)
    valid = lane * SEL_BLOCK <= past
    force = (lane == 0) | (lane >= past // SEL_BLOCK - 1)
    brow = lax.broadcasted_iota(jnp.int32, (128, past), 0)
    kcol = lax.broadcasted_iota(jnp.int32, (128, past), 1)
    expand = jnp.where(brow == kcol // SEL_BLOCK, 1.0, 0.0).astype(BF16)
    all_keys = jnp.full((H, wb), True)

    gs = range(G)
    own = lambda xs: sum(_head_rows_of_group(xs[g], g) for g in gs)
    kw = [token_rows(nwin_ref, wb, 0, g) for g in gs]
    vw = [token_rows(nwin_ref, wb, 1, g) for g in gs]
    kg = [jnp.concatenate([_bf(token_rows(r, page, 0, g)) for r in page_refs], axis=0) for g in gs]
    vg = [jnp.concatenate([_bf(token_rows(r, page, 1, g)) for r in page_refs], axis=0) for g in gs]
    s_win = [_dot_nt(q_rot, kw[g]) * scale for g in gs]
    s_sel = [_dot_nt(q_rot, kg[g]) * scale for g in gs]
    s_new = [jnp.sum(q_rot * new_ref[2 * G + g:2 * G + g + 1, :], axis=1, keepdims=True) * scale for g in gs]
    s_cmp = [_dot_nt(q_raw, kvc_ref[0, 0, g * nseg:(g + 1) * nseg, :]) * scale for g in gs]
    p_cmp = [_masked_softmax(s_cmp[g], c_ok) for g in gs]
    p_win = [_masked_softmax(s_win[g], all_keys) for g in gs]
    o_cmp = own([_dot(p_cmp[g], kvc_ref[0, 1, g * nseg:(g + 1) * nseg, :]) for g in gs])
    o_win = own([_dot(p_win[g], vw[g]) for g in gs])
    ps_rows = [jnp.sum(_head_rows_of_group(p_cmp[g], g), axis=0, keepdims=True) for g in gs]
    imp = sum(jnp.dot(part, cover, preferred_element_type=F32) for part in _split3(_pad_rows(ps_rows, 8)))
    score = jnp.where(valid, jnp.where(force, 1e9, imp), -1.0)
    score = jnp.where(lane < n_sel, score, -2.0)
    sel = _select_blocks_ranked(score, G, N_SELECT)
    picked = jnp.dot(_bf(sel), expand, preferred_element_type=F32)
    o_parts = []
    for g in gs:
        kmask = jnp.broadcast_to(picked[g:g + 1], (H, past)) > 0.5
        new_ok = sel[g:g + 1, n_sel - 1:n_sel] > 0.5
        s = jnp.where(kmask, s_sel[g], NEG_BIG)
        sn = jnp.where(new_ok, s_new[g], NEG_BIG)
        m = jnp.maximum(jnp.max(s, axis=1, keepdims=True), sn)
        e = jnp.where(kmask, jnp.exp(s - m), 0.0)
        e_new = jnp.where(new_ok, jnp.exp(sn - m), 0.0)
        den = jnp.maximum(jnp.sum(e, axis=1, keepdims=True) + e_new, 1e-30)
        o_parts.append((_dot(e, vg[g]) + e_new * new_ref[3 * G + g:3 * G + g + 1, :]) / den)
    o_sel = own(o_parts)

    gates = _sigmoid(gate_ref[0])
    hh = lax.broadcasted_iota(jnp.int32, (H, 128), 0)
    ll = lax.broadcasted_iota(jnp.int32, (H, 128), 1)
    gcol = lambda c: jnp.sum(jnp.where(ll == 3 * hh + c, gates, 0.0), axis=1, keepdims=True)
    o_ref[0] = _bf(gcol(0) * o_cmp + gcol(1) * o_sel + gcol(2) * o_win)


def nsa_step(q, gates, cosf, sinf, kvc, kv_new, cache_sel, cache_win, page_table):
    b, n_pages = page_table.shape
    page = cache_sel.shape[1]
    wb = cache_win.shape[1]
    H, G, D = NSA_HEADS, NSA_GROUPS, HEAD_DIM

    rows_pp, rows_w = page * 2 * G, wb * 2 * G
    cache_sel = cache_sel.reshape(cache_sel.shape[0] * rows_pp, D)
    cache_win2 = cache_win.reshape(b * rows_w, D)

    def page_spec(p):
        return pl.BlockSpec((rows_pp, D), lambda i, pt: (pt[i, p], 0))

    per_seq = lambda *shape: pl.BlockSpec((1,) + shape, lambda i, pt: (i,) + (0,) * len(shape))
    flat_seq = lambda rows: pl.BlockSpec((rows, D), lambda i, pt: (i, 0))
    const = lambda *shape: pl.BlockSpec(shape, lambda i, pt: (0,) * len(shape))
    o, nwin = pl.pallas_call(
        functools.partial(_nsa_step_kernel, n_pages=n_pages),
        name="nsa_step",
        grid_spec=pltpu.PrefetchScalarGridSpec(
            num_scalar_prefetch=1,
            grid=(b,),
            in_specs=[page_spec(p) for p in range(n_pages)] + [
                per_seq(H, D), per_seq(1, 128), const(1, D), const(1, D),
                per_seq(2, kvc.shape[2], D), flat_seq(6 * G), flat_seq(rows_w)],
            out_specs=[per_seq(H, D), flat_seq(rows_w)],
        ),
        out_shape=[jax.ShapeDtypeStruct((b, H, D), BF16), jax.ShapeDtypeStruct((b * rows_w, D), F32)],
        compiler_params=_cparams("parallel"),
    )(page_table, *([cache_sel] * n_pages), q.reshape(b, H, D), gates.reshape(b, 1, 128), cosf, sinf,
      kvc, kv_new.reshape(b * 6 * G, D), cache_win2)
    return o.reshape(b, H * D), nwin.reshape(cache_win.shape)


def kernel(x_prompt, x_sample, state_gdn, state_gdn_conv, state_ffn_conv, cache_cmp_kv, cache_sel_kv, cache_win_kv, page_table, norm_mixer, norm_ffn, norm_kv, norm_final, gdn_w_in, gdn_conv_w, gdn_A_log, gdn_dt_bias, gdn_norm_w, gdn_w_out, nsa_w_q, nsa_w_out, kv_w, cmp_pe_k, cmp_pe_v, cmp_w1_k, cmp_w2_k, cmp_w1_v, cmp_w2_v, ffn_w_gate, ffn_w_up, ffn_conv_w, ffn_conv_b, ffn_w_down):
    cmp_pe = jnp.stack([cmp_pe_k, cmp_pe_v])
    cmp_w1 = jnp.stack([cmp_w1_k, cmp_w1_v])
    cmp_w2 = jnp.stack([cmp_w2_k, cmp_w2_v])

    x = x_prompt[0]
    t = x.shape[0]
    x, gdn_s_p, gdn_c_p = gdn_layer_seq(x, norm_mixer[0], gdn_w_in[0], gdn_conv_w[0], gdn_A_log[0], gdn_dt_bias[0],
                                        gdn_norm_w[0], gdn_w_out[0])
    x, ffn_c0_p = ffn_layer_seq(x, norm_ffn[0], (ffn_w_gate, 0), (ffn_w_up, 0), ffn_conv_w[0], ffn_conv_b[0],
                                (ffn_w_down, 0))
    cosf, sinf = _rope_tables(jnp.arange(t, dtype=jnp.int32))
    kv, kvb = kv_rows(x, norm_kv, kv_w, cosf, sinf)
    kvc = compress_seq(kv, cmp_pe, cmp_w1, cmp_w2)
    x = nsa_layer_seq(x, norm_mixer[1], nsa_w_q[0], nsa_w_out[0], kvb, kvc[0], kvc[1], cosf, sinf)
    y_p, ffn_c1_p = ffn_layer_seq(x, norm_ffn[1], (ffn_w_gate, 1), (ffn_w_up, 1), ffn_conv_w[1], ffn_conv_b[1],
                                  (ffn_w_down, 1), final_nw=norm_final)
    g, hd = NSA_GROUPS, HEAD_DIM
    nrow = 2 * g * hd
    cmp_p = kv[:, :nrow].reshape(1, t, 2, g, hd)
    sel_p = kv[:, nrow:2 * nrow].reshape(1, t, 2, g, hd)
    win_p = kv[t - min(WINDOW, t):, 2 * nrow:].reshape(1, min(WINDOW, t), 2, g, hd)
    assert x_sample.shape[1] == 1
    xs = x_sample[:, 0]
    b = xs.shape[0]
    n_pool, page = cache_sel_kv.shape[:2]
    past_len = page_table.shape[1] * page
    wb = cache_win_kv.shape[1]
    xs, gdn_s_s, gdn_c_s = gdn_layer_step(xs, norm_mixer[0], gdn_w_in[0], gdn_conv_w[0], gdn_A_log[0], gdn_dt_bias[0],
                                          gdn_norm_w[0], gdn_w_out[0], state_gdn_conv[0], state_gdn[0])
    xs, ffn_c0_s = ffn_layer_step(xs, norm_ffn[0], (ffn_w_gate, 0), (ffn_w_up, 0), ffn_conv_w[0], ffn_conv_b[0],
                                  (ffn_w_down, 0), state_ffn_conv[0])
    cos1, sin1 = _rope_tables(jnp.full((1,), past_len, jnp.int32))
    kv_s, _ = kv_rows(xs, norm_kv, kv_w, jnp.broadcast_to(cos1, (b, hd)), jnp.broadcast_to(sin1, (b, hd)))
    kvc_s = compress_pages(cache_cmp_kv, page_table, cmp_pe, _bf(cmp_w1), cmp_w2)
    nq = NSA_HEADS * hd
    q_s = rms_matmul(xs, norm_mixer[1], nsa_w_q[0], nq, 512)
    gates_s = rms_matmul(xs, norm_mixer[1], _pad_cols(nsa_w_q[0][:, nq:], 128), 128, 128)
    o_s, win_s = nsa_step(q_s, gates_s, cos1, sin1, kvc_s, kv_s, cache_sel_kv, cache_win_kv, page_table)
    xs = matmul_res(o_s, nsa_w_out[0], xs)
    y_s, ffn_c1_s = ffn_layer_step(xs, norm_ffn[1], (ffn_w_gate, 1), (ffn_w_up, 1), ffn_conv_w[1], ffn_conv_b[1],
                                   (ffn_w_down, 1), state_ffn_conv[1], final_nw=norm_final)
    cmp_s = kv_s[:, :nrow].reshape(b, 1, 2, g, hd)
    sel_s = kv_s[:, nrow:2 * nrow].reshape(b, 1, 2, g, hd)

    return (y_p[None], y_s[:, None],
            gdn_s_p[None, None], gdn_c_p[None, None], jnp.stack([ffn_c0_p, ffn_c1_p])[:, None], cmp_p, sel_p, win_p,
            gdn_s_s[None], gdn_c_s[None], jnp.stack([ffn_c0_s, ffn_c1_s]), cmp_s, sel_s, win_s)
```

```python
import functools

import jax
import jax.numpy as jnp
import numpy as np
from jax import lax
from jax.experimental import pallas as pl
from jax.experimental.pallas import tpu as pltpu

F32 = jnp.float32
BF16 = jnp.bfloat16

RMS_EPS = 1e-6
ROPE_THETA = 10000.0
HEAD_DIM = 128
GDN_HEADS = 16
GDN_CONV = 4
GDN_CHUNK = 128
NSA_HEADS = 16
NSA_GROUPS = 4
NSA_HPG = NSA_HEADS // NSA_GROUPS
CMP_STRIDE = 16
CMP_BLOCK = 32
SEL_BLOCK = 64
N_SELECT = 16
WINDOW = 512
Q_BLOCK = 128
FFN_CONV = 3
NEG_BIG = -1e30
LOG2E = 1.4426950408889634

VMEM_LIMIT = 52 * 1024 * 1024


def _cparams(*sem):
    return pltpu.CompilerParams(dimension_semantics=sem, vmem_limit_bytes=VMEM_LIMIT)


def _bf(x):
    return x.astype(BF16)


def _dot(a, b):
    return jnp.dot(_bf(a), _bf(b), preferred_element_type=F32)


def _dot_nt(a, b):
    return lax.dot_general(_bf(a), _bf(b), (((1,), (1,)), ((), ())), preferred_element_type=F32)


def _dot_tn(a, b):
    return lax.dot_general(_bf(a), _bf(b), (((0,), (0,)), ((), ())), preferred_element_type=F32)


def _sigmoid(x):
    return 1.0 / (1.0 + jnp.exp(-x))


def _silu(x):
    return x * _sigmoid(x)


def _rms(x, w):
    return x * lax.rsqrt(jnp.mean(x * x, axis=-1, keepdims=True) + RMS_EPS) * w


def _layered(w):
    return w if isinstance(w, tuple) else (w[None], 0)


def _w_spec(block, index_map, layer):
    return pl.BlockSpec((None,) + block, lambda *idx: (layer,) + index_map(*idx))


def _rms_mm_kernel(x_ref, nw_ref, w_ref, o_ref, h_ref):
    @pl.when(pl.program_id(1) == 0)
    def _():
        h_ref[...] = _bf(_rms(x_ref[...], nw_ref[...]))

    o_ref[...] = jnp.dot(h_ref[...], _bf(w_ref[...]), preferred_element_type=F32).astype(o_ref.dtype)


def rms_matmul(x, nw, w, n_out, tn, tm=2048, out_dtype=F32):
    m, k = x.shape
    tm = min(tm, m)
    return pl.pallas_call(
        _rms_mm_kernel,
        name="rms_mm",
        grid=(m // tm, n_out // tn),
        in_specs=[
            pl.BlockSpec((tm, k), lambda i, j: (i, 0), pipeline_mode=pl.Buffered(1)),
            pl.BlockSpec((1, k), lambda i, j: (0, 0)),
            pl.BlockSpec((k, tn), lambda i, j: (0, j)),
        ],
        out_specs=pl.BlockSpec((tm, tn), lambda i, j: (i, j)),
        out_shape=jax.ShapeDtypeStruct((m, n_out), out_dtype),
        scratch_shapes=[pltpu.VMEM((tm, k), BF16)],
        compiler_params=_cparams("parallel", "arbitrary"),
    )(x, nw.reshape(1, k), w)


def _mm_res_kernel(a_ref, w_ref, r_ref, *rest, final_norm):
    if final_norm:
        nw_ref, o_ref = rest
    else:
        (o_ref,) = rest
    kk = pl.program_id(1)

    @pl.when(kk == 0)
    def _():
        o_ref[...] = r_ref[...]

    o_ref[...] += jnp.dot(a_ref[...], _bf(w_ref[...]), preferred_element_type=F32)

    if final_norm:
        @pl.when(kk == pl.num_programs(1) - 1)
        def _():
            o_ref[...] = _rms(o_ref[...], nw_ref[...])


def matmul_res(a, w, res, final_nw=None, tm=1024, tk=512):
    m, k = a.shape
    w, layer = _layered(w)
    n = w.shape[2]
    tm = min(tm, m)
    in_specs = [
        pl.BlockSpec((tm, tk), lambda i, kk: (i, kk)),
        _w_spec((tk, n), lambda i, kk: (kk, 0), layer),
        pl.BlockSpec((tm, n), lambda i, kk: (i, 0)),
    ]
    args = [a, w, res]
    if final_nw is not None:
        in_specs.append(pl.BlockSpec((1, n), lambda i, kk: (0, 0)))
        args.append(final_nw.reshape(1, n))
    return pl.pallas_call(
        functools.partial(_mm_res_kernel, final_norm=final_nw is not None),
        name="mm_res",
        grid=(m // tm, k // tk),
        in_specs=in_specs,
        out_specs=pl.BlockSpec((tm, n), lambda i, kk: (i, 0)),
        out_shape=jax.ShapeDtypeStruct((m, n), F32),
        compiler_params=_cparams("parallel", "arbitrary"),
    )(*args)


def _ffn_up_seq_kernel(x_ref, nw_ref, wg_ref, wu_ref, cw_ref, cb_ref, act_ref, st_ref, h_ref, carry_ref, buf_ref):
    i, j = pl.program_id(0), pl.program_id(1)
    tm = x_ref.shape[0]

    @pl.when(j == 0)
    def _():
        h_ref[...] = _bf(_rms(x_ref[...], nw_ref[...]))

    @pl.when(i == 0)
    def _():
        carry_ref[j] = jnp.zeros(carry_ref.shape[1:], F32)

    h = h_ref[...]
    g = jnp.dot(h, _bf(wg_ref[...]), preferred_element_type=F32)
    u = jnp.dot(h, _bf(wu_ref[...]), preferred_element_type=F32)
    buf_ref[0:8, :] = carry_ref[j]
    buf_ref[8:8 + tm, :] = g
    cw = cw_ref[...]
    a = cw[0:1] * buf_ref[6:6 + tm, :] + cw[1:2] * buf_ref[7:7 + tm, :] + cw[2:3] * g
    act_ref[...] = _bf(_silu(a + cb_ref[...]) * u)
    carry_ref[j] = g[tm - 8:tm]
    st_ref[0] = g[tm - 8:tm]


def ffn_up_seq(x, nw, wg, wu, cw, cb, tm=1024, tn=512):
    m, k = x.shape
    (wg, lg), (wu, lu) = _layered(wg), _layered(wu)
    f = wg.shape[2]
    nb = f // tn
    act, st = pl.pallas_call(
        _ffn_up_seq_kernel,
        name="ffn_up_seq",
        grid=(m // tm, nb),
        in_specs=[
            pl.BlockSpec((tm, k), lambda i, j: (i, 0)),
            pl.BlockSpec((1, k), lambda i, j: (0, 0)),
            _w_spec((k, tn), lambda i, j: (0, j), lg),
            _w_spec((k, tn), lambda i, j: (0, j), lu),
            pl.BlockSpec((FFN_CONV, tn), lambda i, j: (0, j)),
            pl.BlockSpec((1, tn), lambda i, j: (0, j)),
        ],
        out_specs=[
            pl.BlockSpec((tm, tn), lambda i, j: (i, j)),
            pl.BlockSpec((1, 8, tn), lambda i, j: (i, 0, j)),
        ],
        out_shape=[
            jax.ShapeDtypeStruct((m, f), BF16),
            jax.ShapeDtypeStruct((m // tm, 8, f), F32),
        ],
        scratch_shapes=[
            pltpu.VMEM((tm, k), BF16),
            pltpu.VMEM((nb, 8, tn), F32),
            pltpu.VMEM((tm + 8, tn), F32),
        ],
        compiler_params=_cparams("arbitrary", "arbitrary"),
    )(x, nw.reshape(1, k), wg, wu, cw, cb.reshape(1, f))
    return act, st[-1, 8 - (FFN_CONV - 1):]


def _ffn_up_step_kernel(x_ref, nw_ref, wg_ref, wu_ref, cw_ref, cb_ref, p0_ref, p1_ref, act_ref, g_ref, h_ref):
    @pl.when(pl.program_id(1) == 0)
    def _():
        h_ref[...] = _bf(_rms(x_ref[...], nw_ref[...]))

    h = h_ref[...]
    g = jnp.dot(h, _bf(wg_ref[...]), preferred_element_type=F32)
    u = jnp.dot(h, _bf(wu_ref[...]), preferred_element_type=F32)
    cw = cw_ref[...]
    a = cw[0:1] * p0_ref[...] + cw[1:2] * p1_ref[...] + cw[2:3] * g
    act_ref[...] = _bf(_silu(a + cb_ref[...]) * u)
    g_ref[...] = g


def ffn_up_step(x, nw, wg, wu, cw, cb, past, tn=512):
    m, k = x.shape
    (wg, lg), (wu, lu) = _layered(wg), _layered(wu)
    f = wg.shape[2]
    nb = f // tn
    past2 = past.reshape(m, (FFN_CONV - 1) * f)
    act, g = pl.pallas_call(
        _ffn_up_step_kernel,
        name="ffn_up_step",
        grid=(1, nb),
        in_specs=[
            pl.BlockSpec((m, k), lambda i, j: (0, 0)),
            pl.BlockSpec((1, k), lambda i, j: (0, 0)),
            _w_spec((k, tn), lambda i, j: (0, j), lg),
            _w_spec((k, tn), lambda i, j: (0, j), lu),
            pl.BlockSpec((FFN_CONV, tn), lambda i, j: (0, j)),
            pl.BlockSpec((1, tn), lambda i, j: (0, j)),
            pl.BlockSpec((m, tn), lambda i, j: (0, j)),
            pl.BlockSpec((m, tn), lambda i, j: (0, j + nb)),
        ],
        out_specs=[
            pl.BlockSpec((m, tn), lambda i, j: (0, j)),
            pl.BlockSpec((m, tn), lambda i, j: (0, j)),
        ],
        out_shape=[
            jax.ShapeDtypeStruct((m, f), BF16),
            jax.ShapeDtypeStruct((m, f), F32),
        ],
        scratch_shapes=[pltpu.VMEM((m, k), BF16)],
        compiler_params=_cparams("arbitrary", "arbitrary"),
    )(x, nw.reshape(1, k), wg, wu, cw, cb.reshape(1, f), past2, past2)
    return act, jnp.stack([past[:, 1], g], axis=1)


GDN_HPS = 8


def _cumsum_rows(x):
    row = lax.broadcasted_iota(jnp.int32, x.shape, 0)
    s = 1
    while s < x.shape[0]:
        x = x + jnp.where(row >= s, pltpu.roll(x, s, 0), 0.0)
        s *= 2
    return x


def _softplus(x):
    return jnp.maximum(x, 0.0) + jnp.log1p(jnp.exp(-jnp.abs(x)))


def _l2norm(x):
    return x * lax.rsqrt(jnp.sum(x * x, axis=-1, keepdims=True) + 1e-6)


def _gdn_seq_kernel(qp_ref, kp_ref, vp_ref, qh_ref, kh_ref, vh_ref, qpast_ref, kpast_ref, vpast_ref,
                    qw_ref, kw_ref, vw_ref, z_ref, ab_ref, alog_ref, dtb_ref, nw_ref, s0_ref,
                    o_ref, sout_ref, s_scr, buf_ref):
    hg, c = pl.program_id(0), pl.program_id(1)
    C = GDN_CHUNK
    D = HEAD_DIM

    @pl.when(c == 0)
    def _():
        s_scr[...] = s0_ref[...]

    def conv(p_ref, h_ref, past_ref, w_ref):
        buf_ref[0:8, :] = jnp.where(c == 0, past_ref[...], h_ref[...])
        buf_ref[8:8 + C, :] = p_ref[...]
        w = w_ref[...]
        y = (w[0:1] * buf_ref[5:5 + C, :] + w[1:2] * buf_ref[6:6 + C, :]
             + w[2:3] * buf_ref[7:7 + C, :] + w[3:4] * buf_ref[8:8 + C, :])
        return _silu(y)

    qc = conv(qp_ref, qh_ref, qpast_ref, qw_ref)
    kc = conv(kp_ref, kh_ref, kpast_ref, kw_ref)
    vc = conv(vp_ref, vh_ref, vpast_ref, vw_ref)

    ab = ab_ref[...]
    lane = lax.broadcasted_iota(jnp.int32, (C, 128), 1)
    g_all = -jnp.exp(alog_ref[...]) * _softplus(ab + dtb_ref[...])
    gc_all = _cumsum_rows(g_all)
    beta_all = _sigmoid(ab)

    row = lax.broadcasted_iota(jnp.int32, (C, C), 0)
    col = lax.broadcasted_iota(jnp.int32, (C, C), 1)
    eye = (row == col).astype(F32)
    nw = nw_ref[...]

    hs = range(GDN_HPS)
    sl = [slice(j * D, (j + 1) * D) for j in hs]
    gcol = [jnp.sum(jnp.where(lane == hg * GDN_HPS + j, gc_all, 0.0), axis=1, keepdims=True) for j in hs]
    bcol = [jnp.sum(jnp.where(lane == hg * GDN_HPS + j + GDN_HEADS, beta_all, 0.0), axis=1, keepdims=True)
            for j in hs]
    colb = [jnp.broadcast_to(gcol[j], (C, C)) for j in hs]
    dec = [jnp.exp(jnp.minimum(colb[j] - colb[j].T, 0.0)) for j in hs]
    q = [_l2norm(qc[:, sl[j]]) * (D ** -0.5) for j in hs]
    k = [_l2norm(kc[:, sl[j]]) for j in hs]
    kb = [k[j] * bcol[j] for j in hs]
    a_dec = [_dot_nt(kb[j], k[j]) * dec[j] for j in hs]
    qk = [jnp.where(row >= col, _dot_nt(q[j], k[j]) * dec[j], 0.0) for j in hs]
    first = ((row ^ col) == 1) & ((row & 1) == 1)
    tm = [eye - jnp.where(first, a_dec[j], 0.0) for j in hs]
    s = 2
    while s < C:
        msk = ((row // (2 * s)) == (col // (2 * s))) & ((row & s) != 0) & ((col & s) == 0)
        x = [_dot(jnp.where(msk, a_dec[j], 0.0), tm[j]) for j in hs]
        tm = [tm[j] - _dot(tm[j], x[j]) for j in hs]
        s *= 2
    egc = [jnp.exp(gcol[j]) for j in hs]
    uw = [_dot(tm[j], jnp.concatenate([vc[:, sl[j]] * bcol[j], kb[j] * egc[j]], axis=1)) for j in hs]
    r = [_dot(jnp.concatenate([uw[j][:, D:], q[j] * egc[j]], axis=0), s_scr[j]) for j in hs]
    v_new = [uw[j][:, :D] - r[j][:C] for j in hs]
    o = [r[j][C:] + _dot(qk[j], v_new[j]) for j in hs]
    for j in hs:
        g_last = colb[j][C - 1:C, :]
        kdec = k[j] * jnp.exp(g_last[:, 0:1] - gcol[j])
        s_scr[j] = s_scr[j] * jnp.exp(g_last) + _dot_tn(kdec, v_new[j])
        o_ref[:, sl[j]] = _bf(_rms(o[j], nw) * _silu(z_ref[:, sl[j]]))

    @pl.when(c == pl.num_programs(1) - 1)
    def _():
        sout_ref[...] = s_scr[...]


def gdn_seq(proj, ab, conv_w, a_log, dt_bias, norm_w, past8, s0):
    t = proj.shape[0]
    C, W = GDN_CHUNK, GDN_HPS * HEAD_DIM
    nq = GDN_HEADS * HEAD_DIM // W
    pad16 = lambda a: jnp.pad(a.reshape(1, GDN_HEADS), ((0, 0), (0, 128 - GDN_HEADS)))

    def rows(off):
        return pl.BlockSpec((C, W), lambda hg, c: (c, off + hg))

    def halo(off):
        return pl.BlockSpec((8, W), lambda hg, c: (jnp.maximum(c * (C // 8) - 1, 0), off + hg))

    def fixed(rws, off):
        return pl.BlockSpec((rws, W), lambda hg, c: (0, off + hg))

    o, s_out = pl.pallas_call(
        _gdn_seq_kernel,
        name="gdn_seq",
        grid=(nq, t // C),
        in_specs=[rows(0), rows(nq), rows(2 * nq), halo(0), halo(nq), halo(2 * nq),
                  fixed(8, 0), fixed(8, nq), fixed(8, 2 * nq),
                  fixed(GDN_CONV, 0), fixed(GDN_CONV, nq), fixed(GDN_CONV, 2 * nq),
                  rows(3 * nq),
                  pl.BlockSpec((C, 128), lambda hg, c: (c, 0)),
                  pl.BlockSpec((1, 128), lambda hg, c: (0, 0)),
                  pl.BlockSpec((1, 128), lambda hg, c: (0, 0)),
                  pl.BlockSpec((1, HEAD_DIM), lambda hg, c: (0, 0)),
                  pl.BlockSpec((GDN_HPS, HEAD_DIM, HEAD_DIM), lambda hg, c: (hg, 0, 0))],
        out_specs=[pl.BlockSpec((C, W), lambda hg, c: (c, hg)),
                   pl.BlockSpec((GDN_HPS, HEAD_DIM, HEAD_DIM), lambda hg, c: (hg, 0, 0))],
        out_shape=[jax.ShapeDtypeStruct((t, GDN_HEADS * HEAD_DIM), BF16),
                   jax.ShapeDtypeStruct((GDN_HEADS, HEAD_DIM, HEAD_DIM), F32)],
        scratch_shapes=[pltpu.VMEM((GDN_HPS, HEAD_DIM, HEAD_DIM), F32),
                        pltpu.VMEM((C + 8, W), F32)],
        compiler_params=_cparams("parallel", "arbitrary"),
    )(proj, proj, proj, proj, proj, proj, past8, past8, past8,
      conv_w, conv_w, conv_w, proj, ab, pad16(a_log), pad16(dt_bias), norm_w.reshape(1, HEAD_DIM), s0)
    return o, s_out


def _pad_cols(w, n):
    return jnp.pad(w, ((0, 0), (0, n - w.shape[1])))


def gdn_layer_seq(x, nmix, w_in, conv_w, a_log, dt_bias, norm_w, w_out):
    nqkv = 3 * GDN_HEADS * HEAD_DIM
    nz = GDN_HEADS * HEAD_DIM
    proj = rms_matmul(x, nmix, w_in, nqkv + nz, 512)
    ab = rms_matmul(x, nmix, _pad_cols(w_in[:, nqkv + nz:], 128), 128, 128)
    past8 = jnp.zeros((8, nqkv), F32)
    s0 = jnp.zeros((GDN_HEADS, HEAD_DIM, HEAD_DIM), F32)
    o, s_out = gdn_seq(proj, ab, conv_w, a_log, dt_bias, norm_w, past8, s0)
    x1 = matmul_res(o, w_out, x)
    return x1, s_out, proj[-(GDN_CONV - 1):, :nqkv]


def ffn_layer_seq(x, nw, wg, wu, cw, cb, wd, final_nw=None):
    act, st = ffn_up_seq(x, nw, wg, wu, cw, cb)
    return matmul_res(act, wd, x, final_nw=final_nw), st


def ffn_layer_step(x, nw, wg, wu, cw, cb, wd, past, final_nw=None):
    act, st = ffn_up_step(x, nw, wg, wu, cw, cb, past)
    return matmul_res(act, wd, x, final_nw=final_nw), st


KV_SECTION = NSA_GROUPS * HEAD_DIM
KV_ROTARY_SECTIONS = (2, 4)


def _rope_tables(pos):
    half = HEAD_DIM // 2
    inv = jnp.float32(ROPE_THETA) ** (-jnp.arange(half, dtype=F32) / half)
    ang = pos.astype(F32)[:, None] * inv[None, :]
    cos, sin = jnp.cos(ang), jnp.sin(ang)
    return jnp.concatenate([cos, cos], axis=1), jnp.concatenate([-sin, sin], axis=1)


def _rope(x, cosf, sinf):
    return x * cosf + pltpu.roll(x, HEAD_DIM // 2, 1) * sinf


def _kv_rows_kernel(x_ref, nw_ref, w_ref, cos_ref, sin_ref, o_ref, ob_ref, h_ref):
    j = pl.program_id(1)

    @pl.when(j == 0)
    def _():
        h_ref[...] = _bf(_rms(x_ref[...], nw_ref[...]))

    y = jnp.dot(h_ref[...], _bf(w_ref[...]), preferred_element_type=F32)
    is_rot = (j == KV_ROTARY_SECTIONS[0]) | (j == KV_ROTARY_SECTIONS[1])

    @pl.when(is_rot)
    def _():
        cosf, sinf = cos_ref[...], sin_ref[...]
        for g in range(NSA_GROUPS):
            sl = slice(g * HEAD_DIM, (g + 1) * HEAD_DIM)
            yg = _rope(y[:, sl], cosf, sinf)
            o_ref[:, sl] = yg
            ob_ref[:, sl] = _bf(yg)

    @pl.when(jnp.logical_not(is_rot))
    def _():
        o_ref[...] = y
        ob_ref[...] = _bf(y)


def kv_rows(x, nw, w, cosf, sinf, tm=1024):
    m, k = x.shape
    n = w.shape[1]
    tm = min(tm, m)
    return pl.pallas_call(
        _kv_rows_kernel,
        name="kv_rows",
        grid=(m // tm, n // KV_SECTION),
        in_specs=[
            pl.BlockSpec((tm, k), lambda i, j: (i, 0), pipeline_mode=pl.Buffered(1)),
            pl.BlockSpec((1, k), lambda i, j: (0, 0)),
            pl.BlockSpec((k, KV_SECTION), lambda i, j: (0, j)),
            pl.BlockSpec((tm, HEAD_DIM), lambda i, j: (i, 0)),
            pl.BlockSpec((tm, HEAD_DIM), lambda i, j: (i, 0)),
        ],
        out_specs=[pl.BlockSpec((tm, KV_SECTION), lambda i, j: (i, j)),
                   pl.BlockSpec((tm, KV_SECTION), lambda i, j: (i, j))],
        out_shape=[jax.ShapeDtypeStruct((m, n), F32), jax.ShapeDtypeStruct((m, n), BF16)],
        scratch_shapes=[pltpu.VMEM((tm, k), BF16)],
        compiler_params=_cparams("parallel", "arbitrary"),
    )(x, nw.reshape(1, k), w, cosf, sinf)


def _compress_seq_kernel(rows_ref, pe_ref, w1_ref, w2_ref, o_ref, xs_ref, ps_ref):
    nseg = o_ref.shape[2]
    half = CMP_STRIDE * HEAD_DIM
    ps_ref[...] = jnp.zeros(ps_ref.shape, F32)
    for t in range(CMP_STRIDE):
        sl = slice(t * HEAD_DIM, (t + 1) * HEAD_DIM)
        xs_ref[:, sl] = _bf(rows_ref[pl.ds(t, nseg, stride=CMP_STRIDE), :])
        ps_ref[0:1, sl] = pe_ref[0, t:t + 1, :]
        ps_ref[8:9, sl] = pe_ref[0, CMP_STRIDE + t:CMP_STRIDE + t + 1, :]
    xs = xs_ref[...]
    w_lo = _bf(w1_ref[0, 0:half, :])
    w_hi = _bf(w1_ref[0, half:2 * half, :])
    lo = jnp.dot(xs, w_lo, preferred_element_type=F32) + _dot(ps_ref[0:8, :], w_lo)[0:1]
    hi = jnp.dot(xs, w_hi, preferred_element_type=F32) + _dot(ps_ref[8:16, :], w_hi)[0:1]
    hid = _silu(lo + pltpu.roll(hi, nseg - 1, 0))
    o_ref[0, 0] = _bf(_dot(hid, w2_ref[0]))


def compress_seq(kv, pe, w1, w2):
    t = kv.shape[0]
    nseg = t // CMP_STRIDE
    hid = w1.shape[2]
    return pl.pallas_call(
        _compress_seq_kernel,
        name="compress_seq",
        grid=(2, NSA_GROUPS),
        in_specs=[
            pl.BlockSpec((t, HEAD_DIM), lambda a, g: (0, a * NSA_GROUPS + g)),
            pl.BlockSpec((1, CMP_BLOCK, HEAD_DIM), lambda a, g: (a, 0, 0)),
            pl.BlockSpec((1, CMP_BLOCK * HEAD_DIM, hid), lambda a, g: (a, 0, 0)),
            pl.BlockSpec((1, hid, HEAD_DIM), lambda a, g: (a, 0, 0)),
        ],
        out_specs=pl.BlockSpec((1, 1, nseg, HEAD_DIM), lambda a, g: (a, g, 0, 0)),
        out_shape=jax.ShapeDtypeStruct((2, NSA_GROUPS, nseg, HEAD_DIM), BF16),
        scratch_shapes=[pltpu.VMEM((nseg, CMP_STRIDE * HEAD_DIM), BF16),
                        pltpu.VMEM((16, CMP_STRIDE * HEAD_DIM), F32)],
        compiler_params=_cparams("arbitrary", "arbitrary"),
    )(kv, pe, w1, w2)


def _group_rows(ref, g):
    return jnp.concatenate([ref[:, (NSA_HPG * g + h) * HEAD_DIM:(NSA_HPG * g + h + 1) * HEAD_DIM]
                            for h in range(NSA_HPG)], axis=0)


def _masked_softmax(s, mask, exp=jnp.exp):
    s = jnp.where(mask, s, NEG_BIG)
    m = jnp.max(s, axis=1, keepdims=True)
    e = jnp.where(mask, exp(s - m), 0.0)
    return e / jnp.maximum(jnp.sum(e, axis=1, keepdims=True), 1e-30)


def _split3(x):
    hi = _bf(x)
    r = x - hi.astype(F32)
    mid = _bf(r)
    return hi, mid, _bf(r - mid.astype(F32))


def _select_blocks(score, blk, n_pick):
    sel = jnp.zeros(score.shape, F32)
    for _ in range(n_pick):
        mx = jnp.max(score, axis=1, keepdims=True)
        idx = jnp.min(jnp.where(score == mx, blk, 1e9), axis=1, keepdims=True)
        pick = blk == idx
        sel = jnp.where(pick, 1.0, sel)
        score = jnp.where(pick, -3.0, score)
    return sel


def _select_blocks_ranked(score, n_rows, n_pick):
    n = score.shape[1]
    st = score.T
    ii = lax.broadcasted_iota(jnp.int32, (n, n), 0)
    jj = lax.broadcasted_iota(jnp.int32, (n, n), 1)
    rows = []
    for r in range(n_rows):
        mine = jnp.broadcast_to(score[r:r + 1, :], (n, n))
        other = jnp.broadcast_to(st[:, r:r + 1], (n, n))
        beats = jnp.where(other > mine, 1.0, jnp.where((other == mine) & (ii < jj), 1.0, 0.0))
        rank = jnp.sum(beats, axis=0, keepdims=True)
        rows.append(jnp.where(rank < n_pick, 1.0, 0.0))
    return _pad_rows(rows, score.shape[0])


def _nsa_cmp_kernel(q_ref, cos_ref, sin_ref, kc_ref, vc_ref, oc_ref, sel_ref, qrot_ref, *, n_sel):
    i = pl.program_id(0)
    Q = Q_BLOCK
    nc = kc_ref.shape[1]
    scale = HEAD_DIM ** -0.5
    cosf, sinf = cos_ref[...], sin_ref[...]
    for h in range(NSA_HEADS):
        sl = slice(h * HEAD_DIM, (h + 1) * HEAD_DIM)
        qrot_ref[:, sl] = _bf(_rope(q_ref[:, sl], cosf, sinf) * (scale * LOG2E))

    qpos_r = i * Q + lax.broadcasted_iota(jnp.int32, (NSA_HPG * Q, nc), 0) % Q
    c_end = lax.broadcasted_iota(jnp.int32, (NSA_HPG * Q, nc), 1) * CMP_STRIDE + (CMP_BLOCK - 1)
    cmask = c_end <= qpos_r
    ci = lax.broadcasted_iota(jnp.int32, (nc, 128), 0) * CMP_STRIDE
    sj = lax.broadcasted_iota(jnp.int32, (nc, 128), 1) * SEL_BLOCK
    cover = jnp.where((ci < sj + SEL_BLOCK) & (ci + CMP_BLOCK > sj), 1.0, 0.0).astype(BF16)
    qpos = i * Q + lax.broadcasted_iota(jnp.int32, (Q, 128), 0)
    blk = lax.broadcasted_iota(jnp.int32, (Q, 128), 1)
    valid = blk * SEL_BLOCK <= qpos
    force = (blk == 0) | (blk >= qpos // SEL_BLOCK - 1)
    blkf = blk.astype(F32)

    scores = []
    for g in range(NSA_GROUPS):
        p = _masked_softmax(_dot_nt(_group_rows(q_ref, g), kc_ref[g]) * scale, cmask)
        oc = _dot(p, vc_ref[g])
        for h in range(NSA_HPG):
            oc_ref[:, (NSA_HPG * g + h) * HEAD_DIM:(NSA_HPG * g + h + 1) * HEAD_DIM] = oc[h * Q:(h + 1) * Q]
        ps = p[0:Q] + p[Q:2 * Q] + p[2 * Q:3 * Q] + p[3 * Q:4 * Q]
        imp = sum(jnp.dot(part, cover, preferred_element_type=F32) for part in _split3(ps))
        score = jnp.where(valid, jnp.where(force, 1e9, imp), -1.0)
        scores.append(jnp.where(blk < n_sel, score, -2.0))
    picked = _select_blocks(jnp.concatenate(scores, axis=0), jnp.concatenate([blkf] * NSA_GROUPS, axis=0), N_SELECT)
    for g in range(NSA_GROUPS):
        sel_ref[:, g * 128:(g + 1) * 128] = _bf(jnp.where(picked[g * Q:(g + 1) * Q] > 0.5, 0.0, NEG_BIG))


def nsa_cmp(q, cosf, sinf, kc, vc):
    t, d = q.shape
    n_sel = t // SEL_BLOCK
    assert N_SELECT <= n_sel <= 128 and t % Q_BLOCK == 0
    nc = kc.shape[1]
    row = lambda w: pl.BlockSpec((Q_BLOCK, w), lambda i: (i, 0))
    full = pl.BlockSpec((NSA_GROUPS, nc, HEAD_DIM), lambda i: (0, 0, 0))
    return pl.pallas_call(
        functools.partial(_nsa_cmp_kernel, n_sel=n_sel),
        name="nsa_cmp",
        grid=(t // Q_BLOCK,),
        in_specs=[row(d), row(HEAD_DIM), row(HEAD_DIM), full, full],
        out_specs=[row(d), row(NSA_GROUPS * 128), row(d)],
        out_shape=[jax.ShapeDtypeStruct((t, d), F32), jax.ShapeDtypeStruct((t, NSA_GROUPS * 128), BF16),
                   jax.ShapeDtypeStruct((t, d), BF16)],
        compiler_params=_cparams("parallel"),
    )(q, cosf, sinf, kc, vc)


SEL_KEYS = 512


def _nsa_sel_kernel(qi_ref, kt_ref, q_ref, sel_ref, k_ref, v_ref, o_ref, m_scr, l_scr, acc_scr):
    n = pl.program_id(0)
    i, kt = qi_ref[n], kt_ref[n]
    Q = Q_BLOCK
    last = (i * Q + Q - 1) // SEL_KEYS

    @pl.when(kt == 0)
    def _():
        m_scr[...] = jnp.full(m_scr.shape, NEG_BIG, F32)
        l_scr[...] = jnp.zeros(l_scr.shape, F32)
        acc_scr[...] = jnp.zeros(acc_scr.shape, F32)

    def tile(diagonal):
        brow = lax.broadcasted_iota(jnp.int32, (128, SEL_KEYS), 0)
        kcol = lax.broadcasted_iota(jnp.int32, (128, SEL_KEYS), 1)
        expand = jnp.where(brow == kt * (SEL_KEYS // SEL_BLOCK) + kcol // SEL_BLOCK, 1.0, 0.0).astype(BF16)
        if diagonal:
            qpos = i * Q + lax.broadcasted_iota(jnp.int32, (Q, SEL_KEYS), 0)
            kpos = kt * SEL_KEYS + lax.broadcasted_iota(jnp.int32, (Q, SEL_KEYS), 1)
            causal = kpos <= qpos
        gs = range(NSA_GROUPS)
        sl = [slice(g * HEAD_DIM, (g + 1) * HEAD_DIM) for g in gs]
        bias = [jnp.dot(sel_ref[:, g * 128:(g + 1) * 128], expand, preferred_element_type=F32) for g in gs]
        if diagonal:
            bias = [jnp.where(causal, b, NEG_BIG) for b in bias]
        s = [_dot_nt(_group_rows(q_ref, g), k_ref[:, sl[g]]) for g in gs]
        s = [(s[g].reshape(NSA_HPG, Q, SEL_KEYS) + bias[g][None]).reshape(NSA_HPG * Q, SEL_KEYS) for g in gs]
        cols = [slice(c * 128, (c + 1) * 128) for c in range(SEL_KEYS // 128)]
        rep = lambda x: jnp.broadcast_to(x, (NSA_HPG * Q, 128))
        m_old = [m_scr[g] for g in gs]
        m_new = [jnp.maximum(m_old[g], rep(jnp.max(s[g], axis=1, keepdims=True))) for g in gs]
        e = [jnp.concatenate([jnp.exp2(s[g][:, c] - m_new[g]) for c in cols], axis=1) for g in gs]
        pv = [_dot(e[g], v_ref[:, sl[g]]) for g in gs]
        for g in gs:
            alpha = jnp.exp2(m_old[g] - m_new[g])
            l_scr[g] = alpha * l_scr[g] + rep(jnp.sum(e[g], axis=1, keepdims=True))
            acc_scr[g] = alpha * acc_scr[g] + pv[g]
            m_scr[g] = m_new[g]

    pl.when(kt < last)(functools.partial(tile, False))
    pl.when(kt == last)(functools.partial(tile, True))

    @pl.when(kt == last)
    def _():
        for g in range(NSA_GROUPS):
            o = acc_scr[g] / jnp.maximum(l_scr[g], 1e-30)
            for h in range(NSA_HPG):
                o_ref[:, (NSA_HPG * g + h) * HEAD_DIM:(NSA_HPG * g + h + 1) * HEAD_DIM] = o[h * Q:(h + 1) * Q]


def nsa_sel(qrot, sel, kvb):
    t, d = qrot.shape
    pairs = [(i, kt) for i in range(t // Q_BLOCK) for kt in range((i * Q_BLOCK + Q_BLOCK - 1) // SEL_KEYS + 1)]
    qi = jnp.asarray(np.array([p[0] for p in pairs], np.int32))
    kti = jnp.asarray(np.array([p[1] for p in pairs], np.int32))
    qblk = lambda w: pl.BlockSpec((Q_BLOCK, w), lambda n, qi, kti: (qi[n], 0))
    kblk = lambda sec: pl.BlockSpec((SEL_KEYS, KV_SECTION), lambda n, qi, kti: (kti[n], sec))
    return pl.pallas_call(
        _nsa_sel_kernel,
        name="nsa_sel",
        grid_spec=pltpu.PrefetchScalarGridSpec(
            num_scalar_prefetch=2,
            grid=(len(pairs),),
            in_specs=[qblk(d), qblk(NSA_GROUPS * 128), kblk(2), kblk(3)],
            out_specs=qblk(d),
            scratch_shapes=[pltpu.VMEM((NSA_GROUPS, NSA_HPG * Q_BLOCK, 128), F32),
                            pltpu.VMEM((NSA_GROUPS, NSA_HPG * Q_BLOCK, 128), F32),
                            pltpu.VMEM((NSA_GROUPS, NSA_HPG * Q_BLOCK, HEAD_DIM), F32)],
        ),
        out_shape=jax.ShapeDtypeStruct((t, d), F32),
        compiler_params=_cparams("arbitrary"),
    )(qi, kti, qrot, sel, kvb, kvb)


WIN_BLOCKS = WINDOW // Q_BLOCK + 1


def _nsa_win_kernel(q_ref, *refs):
    k_refs, v_refs = refs[:WIN_BLOCKS], refs[WIN_BLOCKS:2 * WIN_BLOCKS]
    oc_ref, os_ref, gate_ref, o_ref = refs[2 * WIN_BLOCKS:]
    i = pl.program_id(0)
    Q = Q_BLOCK
    span = WIN_BLOCKS * Q
    kw = jnp.concatenate([r[...] for r in k_refs], axis=0)
    vw = jnp.concatenate([r[...] for r in v_refs], axis=0)
    qpos = i * Q + lax.broadcasted_iota(jnp.int32, (Q, span), 0)
    kpos = (i - (WIN_BLOCKS - 1)) * Q + lax.broadcasted_iota(jnp.int32, (Q, span), 1)
    bias = jnp.where((kpos <= qpos) & (kpos > qpos - WINDOW) & (kpos >= 0), 0.0, NEG_BIG)
    gates = _sigmoid(gate_ref[...])
    for g in range(NSA_GROUPS):
        sl = slice(g * HEAD_DIM, (g + 1) * HEAD_DIM)
        s = _dot_nt(_group_rows(q_ref, g), kw[:, sl])
        s = (s.reshape(NSA_HPG, Q, span) + bias[None]).reshape(NSA_HPG * Q, span)
        e = jnp.exp2(s - jnp.max(s, axis=1, keepdims=True))
        ow = _dot(e, vw[:, sl]) / jnp.maximum(jnp.sum(e, axis=1, keepdims=True), 1e-30)
        for h in range(NSA_HPG):
            head = NSA_HPG * g + h
            hs = slice(head * HEAD_DIM, (head + 1) * HEAD_DIM)
            o = (gates[:, 3 * head:3 * head + 1] * oc_ref[:, hs] + gates[:, 3 * head + 1:3 * head + 2] * os_ref[:, hs]
                 + gates[:, 3 * head + 2:3 * head + 3] * ow[h * Q:(h + 1) * Q])
            o_ref[:, hs] = _bf(o)


def nsa_win(qrot, kvb, oc, osel, gates):
    t, d = qrot.shape
    row = lambda w: pl.BlockSpec((Q_BLOCK, w), lambda i: (i, 0))
    kblk = lambda sec, b: pl.BlockSpec(
        (Q_BLOCK, KV_SECTION), lambda i: (jnp.maximum(i - (WIN_BLOCKS - 1) + b, 0), sec))
    return pl.pallas_call(
        _nsa_win_kernel,
        name="nsa_win",
        grid=(t // Q_BLOCK,),
        in_specs=([row(d)] + [kblk(4, b) for b in range(WIN_BLOCKS)] + [kblk(5, b) for b in range(WIN_BLOCKS)]
                  + [row(d), row(d), row(128)]),
        out_specs=row(d),
        out_shape=jax.ShapeDtypeStruct((t, d), BF16),
        compiler_params=_cparams("parallel"),
    )(qrot, *([kvb] * (2 * WIN_BLOCKS)), oc, osel, gates)


def nsa_layer_seq(x, nmix, w_q, w_out, kvb, kc, vc, cosf, sinf):
    nq = NSA_HEADS * HEAD_DIM
    q = rms_matmul(x, nmix, w_q, nq, 512)
    gates = rms_matmul(x, nmix, _pad_cols(w_q[:, nq:], 128), 128, 128)
    oc, sel, qrot = nsa_cmp(q, cosf, sinf, kc, vc)
    osel = nsa_sel(qrot, sel, kvb)
    o = nsa_win(qrot, kvb, oc, osel, gates)
    return matmul_res(o, w_out, x)


def _head_column(row, offset, n_heads):
    h = lax.broadcasted_iota(jnp.int32, (n_heads, 128), 0)
    lane = lax.broadcasted_iota(jnp.int32, (n_heads, 128), 1)
    return jnp.sum(jnp.where(lane == h + offset, row, 0.0), axis=1, keepdims=True)


def _pad_rows(rows, n):
    return jnp.concatenate(rows + [jnp.zeros((n - len(rows), rows[0].shape[1]), F32)], axis=0)


def _gdn_step_kernel(proj_ref, ab_ref, past_ref, cw_ref, alog_ref, dtb_ref, nw_ref, s_ref, o_ref, sout_ref):
    H, D = GDN_HEADS, HEAD_DIM
    proj = proj_ref[0]
    cw = cw_ref[...]
    y = cw[GDN_CONV - 1] * proj[0:3 * H]
    for j in range(GDN_CONV - 1):
        y = y + cw[j] * past_ref[0, j]
    y = _silu(y)
    q = _l2norm(y[0:H]) * (D ** -0.5)
    k = _l2norm(y[H:2 * H])
    v = y[2 * H:3 * H]
    ab = ab_ref[0]
    g_row = -jnp.exp(alog_ref[...]) * _softplus(ab + dtb_ref[...])
    eg = jnp.exp(_head_column(g_row, 0, H))
    beta = _head_column(_sigmoid(ab), H, H)
    kb = k * beta
    w = kb * eg
    qe = q * eg
    rs = [_dot(_pad_rows([w[h:h + 1], qe[h:h + 1]], 8), s_ref[0, h]) for h in range(H)]
    ws = jnp.concatenate([r[0:1] for r in rs], axis=0)
    qs = jnp.concatenate([r[1:2] for r in rs], axis=0)
    v_new = v * beta - ws
    o = qs + jnp.sum(q * k, axis=1, keepdims=True) * v_new
    for h in range(H):
        outer = _dot_tn(_pad_rows([k[h:h + 1]], 8), _pad_rows([v_new[h:h + 1]], 8))
        sout_ref[0, h] = s_ref[0, h] * eg[h:h + 1] + outer
    o_ref[0] = _bf(_rms(o, nw_ref[...]) * _silu(proj[3 * H:4 * H]))


def gdn_step(proj, ab, conv_past, conv_w, a_log, dt_bias, norm_w, s0):
    b = proj.shape[0]
    H, D = GDN_HEADS, HEAD_DIM
    pad16 = lambda a: jnp.pad(a.reshape(1, H), ((0, 0), (0, 128 - H)))
    o, s_out = pl.pallas_call(
        _gdn_step_kernel,
        name="gdn_step",
        grid=(b,),
        in_specs=[pl.BlockSpec((1, 4 * H, D), lambda i: (i, 0, 0)),
                  pl.BlockSpec((1, 1, 128), lambda i: (i, 0, 0)),
                  pl.BlockSpec((1, GDN_CONV - 1, 3 * H, D), lambda i: (i, 0, 0, 0)),
                  pl.BlockSpec((GDN_CONV, 3 * H, D), lambda i: (0, 0, 0)),
                  pl.BlockSpec((1, 128), lambda i: (0, 0)),
                  pl.BlockSpec((1, 128), lambda i: (0, 0)),
                  pl.BlockSpec((1, D), lambda i: (0, 0)),
                  pl.BlockSpec((1, H, D, D), lambda i: (i, 0, 0, 0))],
        out_specs=[pl.BlockSpec((1, H, D), lambda i: (i, 0, 0)),
                   pl.BlockSpec((1, H, D, D), lambda i: (i, 0, 0, 0))],
        out_shape=[jax.ShapeDtypeStruct((b, H, D), BF16), jax.ShapeDtypeStruct((b, H, D, D), F32)],
        compiler_params=_cparams("parallel"),
    )(proj.reshape(b, 4 * H, D), ab.reshape(b, 1, 128), conv_past.reshape(b, GDN_CONV - 1, 3 * H, D),
      conv_w.reshape(GDN_CONV, 3 * H, D), pad16(a_log), pad16(dt_bias), norm_w.reshape(1, D), s0)
    return o.reshape(b, H * D), s_out


def gdn_layer_step(x, nmix, w_in, conv_w, a_log, dt_bias, norm_w, w_out, conv_past, s0):
    nqkv = 3 * GDN_HEADS * HEAD_DIM
    nz = GDN_HEADS * HEAD_DIM
    proj = rms_matmul(x, nmix, w_in, nqkv + nz, 512)
    ab = rms_matmul(x, nmix, _pad_cols(w_in[:, nqkv + nz:], 128), 128, 128)
    o, s_out = gdn_step(proj, ab, conv_past, conv_w, a_log, dt_bias, norm_w, s0)
    conv_new = jnp.concatenate([conv_past[:, 1:], proj[:, None, :nqkv]], axis=1)
    return matmul_res(o, w_out, x), s_out, conv_new


def _compress_pages_kernel(pt_ref, *refs, n_pages):
    page_refs = refs[:n_pages]
    pe_ref, w1_ref, w2_ref, o_ref, xs_ref, ps_ref = refs[n_pages:]
    G, D = NSA_GROUPS, HEAD_DIM
    seg_pp = page_refs[0].shape[0]
    nseg = n_pages * seg_pp
    half = CMP_STRIDE * D
    ps_ref[...] = jnp.zeros(ps_ref.shape, F32)
    for a in range(2):
        for t in range(CMP_STRIDE):
            sl = slice(t * D, (t + 1) * D)
            ps_ref[a, 0:1, sl] = pe_ref[a, t:t + 1, :]
            ps_ref[a, 8:9, sl] = pe_ref[a, CMP_STRIDE + t:CMP_STRIDE + t + 1, :]
    for p in range(n_pages):
        for t in range(CMP_STRIDE):
            by_vec = jnp.swapaxes(page_refs[p][:, t], 0, 1)
            for a in range(2):
                for g in range(G):
                    xs_ref[a, g * nseg + p * seg_pp:g * nseg + (p + 1) * seg_pp, t * D:(t + 1) * D] = by_vec[a * G + g]
    for a in range(2):
        xs = _bf(xs_ref[a])
        w_lo = w1_ref[a, 0:half, :]
        w_hi = w1_ref[a, half:2 * half, :]
        lo = jnp.dot(xs, w_lo, preferred_element_type=F32) + _dot(ps_ref[a, 0:8, :], w_lo)[0:1]
        hi = jnp.dot(xs, w_hi, preferred_element_type=F32) + _dot(ps_ref[a, 8:16, :], w_hi)[0:1]
        hi = jnp.concatenate([pltpu.roll(hi[g * nseg:(g + 1) * nseg], nseg - 1, 0) for g in range(G)], axis=0)
        o_ref[0, a] = _bf(_dot(_silu(lo + hi), w2_ref[a]))


def compress_pages(cache, page_table, pe, w1b, w2):
    b, n_pages = page_table.shape
    page = cache.shape[1]
    seg_pp = page // CMP_STRIDE
    nseg = n_pages * seg_pp
    hid = w1b.shape[2]
    cache = cache.reshape(cache.shape[0] * seg_pp, CMP_STRIDE, 2 * NSA_GROUPS, HEAD_DIM)

    def page_spec(p):
        return pl.BlockSpec((seg_pp, CMP_STRIDE, 2 * NSA_GROUPS, HEAD_DIM), lambda i, pt: (pt[i, p], 0, 0, 0))

    const = lambda shape: pl.BlockSpec(shape, lambda i, pt: (0,) * len(shape))
    return pl.pallas_call(
        functools.partial(_compress_pages_kernel, n_pages=n_pages),
        name="compress_pages",
        grid_spec=pltpu.PrefetchScalarGridSpec(
            num_scalar_prefetch=1,
            grid=(b,),
            in_specs=[page_spec(p) for p in range(n_pages)] + [
                const((2, CMP_BLOCK, HEAD_DIM)), const((2, CMP_BLOCK * HEAD_DIM, hid)), const((2, hid, HEAD_DIM))],
            out_specs=pl.BlockSpec((1, 2, NSA_GROUPS * nseg, HEAD_DIM), lambda i, pt: (i, 0, 0, 0)),
            scratch_shapes=[pltpu.VMEM((2, NSA_GROUPS * nseg, CMP_STRIDE * HEAD_DIM), F32),
                            pltpu.VMEM((2, 16, CMP_STRIDE * HEAD_DIM), F32)],
        ),
        out_shape=jax.ShapeDtypeStruct((b, 2, NSA_GROUPS * nseg, HEAD_DIM), BF16),
        compiler_params=_cparams("parallel"),
    )(page_table, *([cache] * n_pages), pe, w1b, w2)


def _head_rows_of_group(x, g):
    h = lax.broadcasted_iota(jnp.int32, x.shape, 0)
    return jnp.where(h // NSA_HPG == g, x, 0.0)


def _nsa_step_kernel(pt_ref, *refs, n_pages):
    page_refs = refs[:n_pages]
    q_ref, gate_ref, cos_ref, sin_ref, kvc_ref, new_ref, win_ref, o_ref, nwin_ref = refs[n_pages:]
    H, G, D = NSA_HEADS, NSA_GROUPS, HEAD_DIM
    page = page_refs[0].shape[0] // (2 * G)
    past = n_pages * page
    nseg = past // CMP_STRIDE
    n_sel = past // SEL_BLOCK + 1
    wb = win_ref.shape[0] // (2 * G)
    scale = D ** -0.5
    q_raw = q_ref[0]
    q_rot = _rope(q_raw, cos_ref[...], sin_ref[...])
    nwin_ref[0:(wb - 1) * 2 * G, :] = win_ref[2 * G:wb * 2 * G, :]
    nwin_ref[(wb - 1) * 2 * G:wb * 2 * G, :] = new_ref[4 * G:6 * G, :]

    def token_rows(ref, n, kv, g):
        return ref[pl.ds(kv * G + g, n, stride=2 * G), :]

    lane = lax.broadcasted_iota(jnp.int32, (8, 128), 1)
    c_ok = lax.broadcasted_iota(jnp.int32, (H, nseg), 1) * CMP_STRIDE + (CMP_BLOCK - 1) <= past
    ci = lax.broadcasted_iota(jnp.int32, (nseg, 128), 0) * CMP_STRIDE
    sj = lax.broadcasted_iota(jnp.int32, (nseg, 128), 1) * SEL_BLOCK
    cover = jnp.where((ci < sj + SEL_BLOCK) & (ci + CMP_BLOCK > sj), 1.0, 0.0).astype(BF16)
    valid = lane * SEL_BLOCK <= past
    force = (lane == 0) | (lane >= past // SEL_BLOCK - 1)
    brow = lax.broadcasted_iota(jnp.int32, (128, past), 0)
    kcol = lax.broadcasted_iota(jnp.int32, (128, past), 1)
    expand = jnp.where(brow == kcol // SEL_BLOCK, 1.0, 0.0).astype(BF16)
    all_keys = jnp.full((H, wb), True)

    gs = range(G)
    own = lambda xs: sum(_head_rows_of_group(xs[g], g) for g in gs)
    kw = [token_rows(nwin_ref, wb, 0, g) for g in gs]
    vw = [token_rows(nwin_ref, wb, 1, g) for g in gs]
    kg = [jnp.concatenate([_bf(token_rows(r, page, 0, g)) for r in page_refs], axis=0) for g in gs]
    vg = [jnp.concatenate([_bf(token_rows(r, page, 1, g)) for r in page_refs], axis=0) for g in gs]
    s_win = [_dot_nt(q_rot, kw[g]) * scale for g in gs]
    s_sel = [_dot_nt(q_rot, kg[g]) * scale for g in gs]
    s_new = [jnp.sum(q_rot * new_ref[2 * G + g:2 * G + g + 1, :], axis=1, keepdims=True) * scale for g in gs]
    s_cmp = [_dot_nt(q_raw, kvc_ref[0, 0, g * nseg:(g + 1) * nseg, :]) * scale for g in gs]
    p_cmp = [_masked_softmax(s_cmp[g], c_ok) for g in gs]
    p_win = [_masked_softmax(s_win[g], all_keys) for g in gs]
    o_cmp = own([_dot(p_cmp[g], kvc_ref[0, 1, g * nseg:(g + 1) * nseg, :]) for g in gs])
    o_win = own([_dot(p_win[g], vw[g]) for g in gs])
    ps_rows = [jnp.sum(_head_rows_of_group(p_cmp[g], g), axis=0, keepdims=True) for g in gs]
    imp = sum(jnp.dot(part, cover, preferred_element_type=F32) for part in _split3(_pad_rows(ps_rows, 8)))
    score = jnp.where(valid, jnp.where(force, 1e9, imp), -1.0)
    score = jnp.where(lane < n_sel, score, -2.0)
    sel = _select_blocks_ranked(score, G, N_SELECT)
    picked = jnp.dot(_bf(sel), expand, preferred_element_type=F32)
    o_parts = []
    for g in gs:
        kmask = jnp.broadcast_to(picked[g:g + 1], (H, past)) > 0.5
        new_ok = sel[g:g + 1, n_sel - 1:n_sel] > 0.5
        s = jnp.where(kmask, s_sel[g], NEG_BIG)
        sn = jnp.where(new_ok, s_new[g], NEG_BIG)
        m = jnp.maximum(jnp.max(s, axis=1, keepdims=True), sn)
        e = jnp.where(kmask, jnp.exp(s - m), 0.0)
        e_new = jnp.where(new_ok, jnp.exp(sn - m), 0.0)
        den = jnp.maximum(jnp.sum(e, axis=1, keepdims=True) + e_new, 1e-30)
        o_parts.append((_dot(e, vg[g]) + e_new * new_ref[3 * G + g:3 * G + g + 1, :]) / den)
    o_sel = own(o_parts)

    gates = _sigmoid(gate_ref[0])
    hh = lax.broadcasted_iota(jnp.int32, (H, 128), 0)
    ll = lax.broadcasted_iota(jnp.int32, (H, 128), 1)
    gcol = lambda c: jnp.sum(jnp.where(ll == 3 * hh + c, gates, 0.0), axis=1, keepdims=True)
    o_ref[0] = _bf(gcol(0) * o_cmp + gcol(1) * o_sel + gcol(2) * o_win)


def nsa_step(q, gates, cosf, sinf, kvc, kv_new, cache_sel, cache_win, page_table):
    b, n_pages = page_table.shape
    page = cache_sel.shape[1]
    wb = cache_win.shape[1]
    H, G, D = NSA_HEADS, NSA_GROUPS, HEAD_DIM

    rows_pp, rows_w = page * 2 * G, wb * 2 * G
    cache_sel = cache_sel.reshape(cache_sel.shape[0] * rows_pp, D)
    cache_win2 = cache_win.reshape(b * rows_w, D)

    def page_spec(p):
        return pl.BlockSpec((rows_pp, D), lambda i, pt: (pt[i, p], 0))

    per_seq = lambda *shape: pl.BlockSpec((1,) + shape, lambda i, pt: (i,) + (0,) * len(shape))
    flat_seq = lambda rows: pl.BlockSpec((rows, D), lambda i, pt: (i, 0))
    const = lambda *shape: pl.BlockSpec(shape, lambda i, pt: (0,) * len(shape))
    o, nwin = pl.pallas_call(
        functools.partial(_nsa_step_kernel, n_pages=n_pages),
        name="nsa_step",
        grid_spec=pltpu.PrefetchScalarGridSpec(
            num_scalar_prefetch=1,
            grid=(b,),
            in_specs=[page_spec(p) for p in range(n_pages)] + [
                per_seq(H, D), per_seq(1, 128), const(1, D), const(1, D),
                per_seq(2, kvc.shape[2], D), flat_seq(6 * G), flat_seq(rows_w)],
            out_specs=[per_seq(H, D), flat_seq(rows_w)],
        ),
        out_shape=[jax.ShapeDtypeStruct((b, H, D), BF16), jax.ShapeDtypeStruct((b * rows_w, D), F32)],
        compiler_params=_cparams("parallel"),
    )(page_table, *([cache_sel] * n_pages), q.reshape(b, H, D), gates.reshape(b, 1, 128), cosf, sinf,
      kvc, kv_new.reshape(b * 6 * G, D), cache_win2)
    return o.reshape(b, H * D), nwin.reshape(cache_win.shape)


def kernel(x_prompt, x_sample, state_gdn, state_gdn_conv, state_ffn_conv, cache_cmp_kv, cache_sel_kv, cache_win_kv, page_table, norm_mixer, norm_ffn, norm_kv, norm_final, gdn_w_in, gdn_conv_w, gdn_A_log, gdn_dt_bias, gdn_norm_w, gdn_w_out, nsa_w_q, nsa_w_out, kv_w, cmp_pe_k, cmp_pe_v, cmp_w1_k, cmp_w2_k, cmp_w1_v, cmp_w2_v, ffn_w_gate, ffn_w_up, ffn_conv_w, ffn_conv_b, ffn_w_down):
    cmp_pe = jnp.stack([cmp_pe_k, cmp_pe_v])
    cmp_w1 = jnp.stack([cmp_w1_k, cmp_w1_v])
    cmp_w2 = jnp.stack([cmp_w2_k, cmp_w2_v])

    x = x_prompt[0]
    t = x.shape[0]
    x, gdn_s_p, gdn_c_p = gdn_layer_seq(x, norm_mixer[0], gdn_w_in[0], gdn_conv_w[0], gdn_A_log[0], gdn_dt_bias[0],
                                        gdn_norm_w[0], gdn_w_out[0])
    x, ffn_c0_p = ffn_layer_seq(x, norm_ffn[0], (ffn_w_gate, 0), (ffn_w_up, 0), ffn_conv_w[0], ffn_conv_b[0],
                                (ffn_w_down, 0))
    cosf, sinf = _rope_tables(jnp.arange(t, dtype=jnp.int32))
    kv, kvb = kv_rows(x, norm_kv, kv_w, cosf, sinf)
    kvc = compress_seq(kv, cmp_pe, cmp_w1, cmp_w2)
    x = nsa_layer_seq(x, norm_mixer[1], nsa_w_q[0], nsa_w_out[0], kvb, kvc[0], kvc[1], cosf, sinf)
    y_p, ffn_c1_p = ffn_layer_seq(x, norm_ffn[1], (ffn_w_gate, 1), (ffn_w_up, 1), ffn_conv_w[1], ffn_conv_b[1],
                                  (ffn_w_down, 1), final_nw=norm_final)
    g, hd = NSA_GROUPS, HEAD_DIM
    nrow = 2 * g * hd
    cmp_p = kv[:, :nrow].reshape(1, t, 2, g, hd)
    sel_p = kv[:, nrow:2 * nrow].reshape(1, t, 2, g, hd)
    win_p = kv[t - min(WINDOW, t):, 2 * nrow:].reshape(1, min(WINDOW, t), 2, g, hd)
    assert x_sample.shape[1] == 1
    xs = x_sample[:, 0]
    b = xs.shape[0]
    n_pool, page = cache_sel_kv.shape[:2]
    past_len = page_table.shape[1] * page
    wb = cache_win_kv.shape[1]
    xs, gdn_s_s, gdn_c_s = gdn_layer_step(xs, norm_mixer[0], gdn_w_in[0], gdn_conv_w[0], gdn_A_log[0], gdn_dt_bias[0],
                                          gdn_norm_w[0], gdn_w_out[0], state_gdn_conv[0], state_gdn[0])
    xs, ffn_c0_s = ffn_layer_step(xs, norm_ffn[0], (ffn_w_gate, 0), (ffn_w_up, 0), ffn_conv_w[0], ffn_conv_b[0],
                                  (ffn_w_down, 0), state_ffn_conv[0])
    cos1, sin1 = _rope_tables(jnp.full((1,), past_len, jnp.int32))
    kv_s, _ = kv_rows(xs, norm_kv, kv_w, jnp.broadcast_to(cos1, (b, hd)), jnp.broadcast_to(sin1, (b, hd)))
    kvc_s = compress_pages(cache_cmp_kv, page_table, cmp_pe, _bf(cmp_w1), cmp_w2)
    nq = NSA_HEADS * hd
    q_s = rms_matmul(xs, norm_mixer[1], nsa_w_q[0], nq, 512)
    gates_s = rms_matmul(xs, norm_mixer[1], _pad_cols(nsa_w_q[0][:, nq:], 128), 128, 128)
    o_s, win_s = nsa_step(q_s, gates_s, cos1, sin1, kvc_s, kv_s, cache_sel_kv, cache_win_kv, page_table)
    xs = matmul_res(o_s, nsa_w_out[0], xs)
    y_s, ffn_c1_s = ffn_layer_step(xs, norm_ffn[1], (ffn_w_gate, 1), (ffn_w_up, 1), ffn_conv_w[1], ffn_conv_b[1],
                                   (ffn_w_down, 1), state_ffn_conv[1], final_nw=norm_final)
    cmp_s = kv_s[:, :nrow].reshape(b, 1, 2, g, hd)
    sel_s = kv_s[:, nrow:2 * nrow].reshape(b, 1, 2, g, hd)

    return (y_p[None], y_s[:, None],
            gdn_s_p[None, None], gdn_c_p[None, None], jnp.stack([ffn_c0_p, ffn_c1_p])[:, None], cmp_p, sel_p, win_p,
            gdn_s_s[None], gdn_c_s[None], jnp.stack([ffn_c0_s, ffn_c1_s]), cmp_s, sel_s, win_s)
```

```python
import functools

import jax
import jax.numpy as jnp
import numpy as np
from jax import lax
from jax.experimental import pallas as pl
from jax.experimental.pallas import tpu as pltpu

F32 = jnp.float32
BF16 = jnp.bfloat16

RMS_EPS = 1e-6
ROPE_THETA = 10000.0
HEAD_DIM = 128
GDN_HEADS = 16
GDN_CONV = 4
GDN_CHUNK = 128
NSA_HEADS = 16
NSA_GROUPS = 4
NSA_HPG = NSA_HEADS // NSA_GROUPS
CMP_STRIDE = 16
CMP_BLOCK = 32
SEL_BLOCK = 64
N_SELECT = 16
WINDOW = 512
Q_BLOCK = 128
FFN_CONV = 3
NEG_BIG = -1e30
LOG2E = 1.4426950408889634

VMEM_LIMIT = 52 * 1024 * 1024


def _cparams(*sem):
    return pltpu.CompilerParams(dimension_semantics=sem, vmem_limit_bytes=VMEM_LIMIT)


def _bf(x):
    return x.astype(BF16)


def _dot(a, b):
    return jnp.dot(_bf(a), _bf(b), preferred_element_type=F32)


def _dot_nt(a, b):
    return lax.dot_general(_bf(a), _bf(b), (((1,), (1,)), ((), ())), preferred_element_type=F32)


def _dot_tn(a, b):
    return lax.dot_general(_bf(a), _bf(b), (((0,), (0,)), ((), ())), preferred_element_type=F32)


def _sigmoid(x):
    return 1.0 / (1.0 + jnp.exp(-x))


def _silu(x):
    return x * _sigmoid(x)


def _rms(x, w):
    return x * lax.rsqrt(jnp.mean(x * x, axis=-1, keepdims=True) + RMS_EPS) * w


def _layered(w):
    return w if isinstance(w, tuple) else (w[None], 0)


def _w_spec(block, index_map, layer):
    return pl.BlockSpec((None,) + block, lambda *idx: (layer,) + index_map(*idx))


def _rms_mm_kernel(x_ref, nw_ref, w_ref, o_ref, h_ref):
    @pl.when(pl.program_id(1) == 0)
    def _():
        h_ref[...] = _bf(_rms(x_ref[...], nw_ref[...]))

    o_ref[...] = jnp.dot(h_ref[...], _bf(w_ref[...]), preferred_element_type=F32).astype(o_ref.dtype)


def rms_matmul(x, nw, w, n_out, tn, tm=2048, out_dtype=F32):
    m, k = x.shape
    wide = n_out // tn >= 4
    tm = min(tm if wide else tm // 2, m)
    return pl.pallas_call(
        _rms_mm_kernel,
        name="rms_mm",
        grid=(m // tm, n_out // tn),
        in_specs=[
            pl.BlockSpec((tm, k), lambda i, j: (i, 0), pipeline_mode=pl.Buffered(1 if wide else 2)),
            pl.BlockSpec((1, k), lambda i, j: (0, 0)),
            pl.BlockSpec((k, tn), lambda i, j: (0, j)),
        ],
        out_specs=pl.BlockSpec((tm, tn), lambda i, j: (i, j)),
        out_shape=jax.ShapeDtypeStruct((m, n_out), out_dtype),
        scratch_shapes=[pltpu.VMEM((tm, k), BF16)],
        compiler_params=_cparams("parallel", "arbitrary"),
    )(x, nw.reshape(1, k), w)


def _mm_res_kernel(a_ref, w_ref, r_ref, *rest, final_norm):
    if final_norm:
        nw_ref, o_ref = rest
    else:
        (o_ref,) = rest
    kk = pl.program_id(1)

    @pl.when(kk == 0)
    def _():
        o_ref[...] = r_ref[...]

    o_ref[...] += jnp.dot(a_ref[...], _bf(w_ref[...]), preferred_element_type=F32)

    if final_norm:
        @pl.when(kk == pl.num_programs(1) - 1)
        def _():
            o_ref[...] = _rms(o_ref[...], nw_ref[...])


def matmul_res(a, w, res, final_nw=None, tm=1024, tk=512):
    m, k = a.shape
    w, layer = _layered(w)
    n = w.shape[2]
    tm = min(tm, m)
    in_specs = [
        pl.BlockSpec((tm, tk), lambda i, kk: (i, kk)),
        _w_spec((tk, n), lambda i, kk: (kk, 0), layer),
        pl.BlockSpec((tm, n), lambda i, kk: (i, 0)),
    ]
    args = [a, w, res]
    if final_nw is not None:
        in_specs.append(pl.BlockSpec((1, n), lambda i, kk: (0, 0)))
        args.append(final_nw.reshape(1, n))
    return pl.pallas_call(
        functools.partial(_mm_res_kernel, final_norm=final_nw is not None),
        name="mm_res",
        grid=(m // tm, k // tk),
        in_specs=in_specs,
        out_specs=pl.BlockSpec((tm, n), lambda i, kk: (i, 0)),
        out_shape=jax.ShapeDtypeStruct((m, n), F32),
        compiler_params=_cparams("parallel", "arbitrary"),
    )(*args)


def _ffn_up_seq_kernel(x_ref, nw_ref, wg_ref, wu_ref, cw_ref, cb_ref, act_ref, st_ref, h_ref, carry_ref, buf_ref):
    i, j = pl.program_id(0), pl.program_id(1)
    tm = x_ref.shape[0]

    @pl.when(j == 0)
    def _():
        h_ref[...] = _bf(_rms(x_ref[...], nw_ref[...]))

    @pl.when(i == 0)
    def _():
        carry_ref[j] = jnp.zeros(carry_ref.shape[1:], F32)

    h = h_ref[...]
    g = jnp.dot(h, _bf(wg_ref[...]), preferred_element_type=F32)
    u = jnp.dot(h, _bf(wu_ref[...]), preferred_element_type=F32)
    buf_ref[0:8, :] = carry_ref[j]
    buf_ref[8:8 + tm, :] = g
    cw = cw_ref[...]
    a = cw[0:1] * buf_ref[6:6 + tm, :] + cw[1:2] * buf_ref[7:7 + tm, :] + cw[2:3] * g
    act_ref[...] = _bf(_silu(a + cb_ref[...]) * u)
    carry_ref[j] = g[tm - 8:tm]
    st_ref[0] = g[tm - 8:tm]


def ffn_up_seq(x, nw, wg, wu, cw, cb, tm=1024, tn=512):
    m, k = x.shape
    (wg, lg), (wu, lu) = _layered(wg), _layered(wu)
    f = wg.shape[2]
    nb = f // tn
    act, st = pl.pallas_call(
        _ffn_up_seq_kernel,
        name="ffn_up_seq",
        grid=(m // tm, nb),
        in_specs=[
            pl.BlockSpec((tm, k), lambda i, j: (i, 0)),
            pl.BlockSpec((1, k), lambda i, j: (0, 0)),
            _w_spec((k, tn), lambda i, j: (0, j), lg),
            _w_spec((k, tn), lambda i, j: (0, j), lu),
            pl.BlockSpec((FFN_CONV, tn), lambda i, j: (0, j)),
            pl.BlockSpec((1, tn), lambda i, j: (0, j)),
        ],
        out_specs=[
            pl.BlockSpec((tm, tn), lambda i, j: (i, j)),
            pl.BlockSpec((1, 8, tn), lambda i, j: (i, 0, j)),
        ],
        out_shape=[
            jax.ShapeDtypeStruct((m, f), BF16),
            jax.ShapeDtypeStruct((m // tm, 8, f), F32),
        ],
        scratch_shapes=[
            pltpu.VMEM((tm, k), BF16),
            pltpu.VMEM((nb, 8, tn), F32),
            pltpu.VMEM((tm + 8, tn), F32),
        ],
        compiler_params=_cparams("arbitrary", "arbitrary"),
    )(x, nw.reshape(1, k), wg, wu, cw, cb.reshape(1, f))
    return act, st[-1, 8 - (FFN_CONV - 1):]


def _ffn_up_step_kernel(x_ref, nw_ref, wg_ref, wu_ref, cw_ref, cb_ref, p0_ref, p1_ref, act_ref, g_ref, h_ref):
    @pl.when(pl.program_id(1) == 0)
    def _():
        h_ref[...] = _bf(_rms(x_ref[...], nw_ref[...]))

    h = h_ref[...]
    g = jnp.dot(h, _bf(wg_ref[...]), preferred_element_type=F32)
    u = jnp.dot(h, _bf(wu_ref[...]), preferred_element_type=F32)
    cw = cw_ref[...]
    a = cw[0:1] * p0_ref[...] + cw[1:2] * p1_ref[...] + cw[2:3] * g
    act_ref[...] = _bf(_silu(a + cb_ref[...]) * u)
    g_ref[...] = g


def ffn_up_step(x, nw, wg, wu, cw, cb, past, tn=512):
    m, k = x.shape
    (wg, lg), (wu, lu) = _layered(wg), _layered(wu)
    f = wg.shape[2]
    nb = f // tn
    past2 = past.reshape(m, (FFN_CONV - 1) * f)
    act, g = pl.pallas_call(
        _ffn_up_step_kernel,
        name="ffn_up_step",
        grid=(1, nb),
        in_specs=[
            pl.BlockSpec((m, k), lambda i, j: (0, 0)),
            pl.BlockSpec((1, k), lambda i, j: (0, 0)),
            _w_spec((k, tn), lambda i, j: (0, j), lg),
            _w_spec((k, tn), lambda i, j: (0, j), lu),
            pl.BlockSpec((FFN_CONV, tn), lambda i, j: (0, j)),
            pl.BlockSpec((1, tn), lambda i, j: (0, j)),
            pl.BlockSpec((m, tn), lambda i, j: (0, j)),
            pl.BlockSpec((m, tn), lambda i, j: (0, j + nb)),
        ],
        out_specs=[
            pl.BlockSpec((m, tn), lambda i, j: (0, j)),
            pl.BlockSpec((m, tn), lambda i, j: (0, j)),
        ],
        out_shape=[
            jax.ShapeDtypeStruct((m, f), BF16),
            jax.ShapeDtypeStruct((m, f), F32),
        ],
        scratch_shapes=[pltpu.VMEM((m, k), BF16)],
        compiler_params=_cparams("arbitrary", "arbitrary"),
    )(x, nw.reshape(1, k), wg, wu, cw, cb.reshape(1, f), past2, past2)
    return act, jnp.stack([past[:, 1], g], axis=1)


GDN_HPS = 8


def _cumsum_rows(x):
    row = lax.broadcasted_iota(jnp.int32, x.shape, 0)
    s = 1
    while s < x.shape[0]:
        x = x + jnp.where(row >= s, pltpu.roll(x, s, 0), 0.0)
        s *= 2
    return x


def _softplus(x):
    return jnp.maximum(x, 0.0) + jnp.log1p(jnp.exp(-jnp.abs(x)))


def _l2norm(x):
    return x * lax.rsqrt(jnp.sum(x * x, axis=-1, keepdims=True) + 1e-6)


def _gdn_seq_kernel(qp_ref, kp_ref, vp_ref, qh_ref, kh_ref, vh_ref, qpast_ref, kpast_ref, vpast_ref,
                    qw_ref, kw_ref, vw_ref, z_ref, ab_ref, alog_ref, dtb_ref, nw_ref, s0_ref,
                    o_ref, sout_ref, s_scr, buf_ref):
    hg, c = pl.program_id(0), pl.program_id(1)
    C = GDN_CHUNK
    D = HEAD_DIM

    @pl.when(c == 0)
    def _():
        s_scr[...] = s0_ref[...]

    def conv(p_ref, h_ref, past_ref, w_ref):
        buf_ref[0:8, :] = jnp.where(c == 0, past_ref[...], h_ref[...])
        buf_ref[8:8 + C, :] = p_ref[...]
        w = w_ref[...]
        y = (w[0:1] * buf_ref[5:5 + C, :] + w[1:2] * buf_ref[6:6 + C, :]
             + w[2:3] * buf_ref[7:7 + C, :] + w[3:4] * buf_ref[8:8 + C, :])
        return _silu(y)

    qc = conv(qp_ref, qh_ref, qpast_ref, qw_ref)
    kc = conv(kp_ref, kh_ref, kpast_ref, kw_ref)
    vc = conv(vp_ref, vh_ref, vpast_ref, vw_ref)

    ab = ab_ref[...]
    lane = lax.broadcasted_iota(jnp.int32, (C, 128), 1)
    g_all = -jnp.exp(alog_ref[...]) * _softplus(ab + dtb_ref[...])
    gc_all = _cumsum_rows(g_all)
    beta_all = _sigmoid(ab)

    row = lax.broadcasted_iota(jnp.int32, (C, C), 0)
    col = lax.broadcasted_iota(jnp.int32, (C, C), 1)
    eye = (row == col).astype(F32)
    nw = nw_ref[...]

    hs = range(GDN_HPS)
    sl = [slice(j * D, (j + 1) * D) for j in hs]
    gcol = [jnp.sum(jnp.where(lane == hg * GDN_HPS + j, gc_all, 0.0), axis=1, keepdims=True) for j in hs]
    bcol = [jnp.sum(jnp.where(lane == hg * GDN_HPS + j + GDN_HEADS, beta_all, 0.0), axis=1, keepdims=True)
            for j in hs]
    colb = [jnp.broadcast_to(gcol[j], (C, C)) for j in hs]
    dec = [jnp.exp(jnp.minimum(colb[j] - colb[j].T, 0.0)) for j in hs]
    q = [_l2norm(qc[:, sl[j]]) * (D ** -0.5) for j in hs]
    k = [_l2norm(kc[:, sl[j]]) for j in hs]
    kb = [k[j] * bcol[j] for j in hs]
    a_dec = [_dot_nt(kb[j], k[j]) * dec[j] for j in hs]
    qk = [jnp.where(row >= col, _dot_nt(q[j], k[j]) * dec[j], 0.0) for j in hs]
    first = ((row ^ col) == 1) & ((row & 1) == 1)
    tm = [eye - jnp.where(first, a_dec[j], 0.0) for j in hs]
    s = 2
    while s < C:
        msk = ((row // (2 * s)) == (col // (2 * s))) & ((row & s) != 0) & ((col & s) == 0)
        x = [_dot(jnp.where(msk, a_dec[j], 0.0), tm[j]) for j in hs]
        tm = [tm[j] - _dot(tm[j], x[j]) for j in hs]
        s *= 2
    egc = [jnp.exp(gcol[j]) for j in hs]
    uw = [_dot(tm[j], jnp.concatenate([vc[:, sl[j]] * bcol[j], kb[j] * egc[j]], axis=1)) for j in hs]
    r = [_dot(jnp.concatenate([uw[j][:, D:], q[j] * egc[j]], axis=0), s_scr[j]) for j in hs]
    v_new = [uw[j][:, :D] - r[j][:C] for j in hs]
    o = [r[j][C:] + _dot(qk[j], v_new[j]) for j in hs]
    for j in hs:
        g_last = colb[j][C - 1:C, :]
        kdec = k[j] * jnp.exp(g_last[:, 0:1] - gcol[j])
        s_scr[j] = s_scr[j] * jnp.exp(g_last) + _dot_tn(kdec, v_new[j])
        o_ref[:, sl[j]] = _bf(_rms(o[j], nw) * _silu(z_ref[:, sl[j]]))

    @pl.when(c == pl.num_programs(1) - 1)
    def _():
        sout_ref[...] = s_scr[...]


def gdn_seq(proj, ab, conv_w, a_log, dt_bias, norm_w, past8, s0):
    t = proj.shape[0]
    C, W = GDN_CHUNK, GDN_HPS * HEAD_DIM
    nq = GDN_HEADS * HEAD_DIM // W
    pad16 = lambda a: jnp.pad(a.reshape(1, GDN_HEADS), ((0, 0), (0, 128 - GDN_HEADS)))

    def rows(off):
        return pl.BlockSpec((C, W), lambda hg, c: (c, off + hg))

    def halo(off):
        return pl.BlockSpec((8, W), lambda hg, c: (jnp.maximum(c * (C // 8) - 1, 0), off + hg))

    def fixed(rws, off):
        return pl.BlockSpec((rws, W), lambda hg, c: (0, off + hg))

    o, s_out = pl.pallas_call(
        _gdn_seq_kernel,
        name="gdn_seq",
        grid=(nq, t // C),
        in_specs=[rows(0), rows(nq), rows(2 * nq), halo(0), halo(nq), halo(2 * nq),
                  fixed(8, 0), fixed(8, nq), fixed(8, 2 * nq),
                  fixed(GDN_CONV, 0), fixed(GDN_CONV, nq), fixed(GDN_CONV, 2 * nq),
                  rows(3 * nq),
                  pl.BlockSpec((C, 128), lambda hg, c: (c, 0)),
                  pl.BlockSpec((1, 128), lambda hg, c: (0, 0)),
                  pl.BlockSpec((1, 128), lambda hg, c: (0, 0)),
                  pl.BlockSpec((1, HEAD_DIM), lambda hg, c: (0, 0)),
                  pl.BlockSpec((GDN_HPS, HEAD_DIM, HEAD_DIM), lambda hg, c: (hg, 0, 0))],
        out_specs=[pl.BlockSpec((C, W), lambda hg, c: (c, hg)),
                   pl.BlockSpec((GDN_HPS, HEAD_DIM, HEAD_DIM), lambda hg, c: (hg, 0, 0))],
        out_shape=[jax.ShapeDtypeStruct((t, GDN_HEADS * HEAD_DIM), BF16),
                   jax.ShapeDtypeStruct((GDN_HEADS, HEAD_DIM, HEAD_DIM), F32)],
        scratch_shapes=[pltpu.VMEM((GDN_HPS, HEAD_DIM, HEAD_DIM), F32),
                        pltpu.VMEM((C + 8, W), F32)],
        compiler_params=_cparams("parallel", "arbitrary"),
    )(proj, proj, proj, proj, proj, proj, past8, past8, past8,
      conv_w, conv_w, conv_w, proj, ab, pad16(a_log), pad16(dt_bias), norm_w.reshape(1, HEAD_DIM), s0)
    return o, s_out


def _pad_cols(w, n):
    return jnp.pad(w, ((0, 0), (0, n - w.shape[1])))


def gdn_layer_seq(x, nmix, w_in, conv_w, a_log, dt_bias, norm_w, w_out):
    nqkv = 3 * GDN_HEADS * HEAD_DIM
    nz = GDN_HEADS * HEAD_DIM
    proj = rms_matmul(x, nmix, w_in, nqkv + nz, 512)
    ab = rms_matmul(x, nmix, _pad_cols(w_in[:, nqkv + nz:], 128), 128, 128)
    past8 = jnp.zeros((8, nqkv), F32)
    s0 = jnp.zeros((GDN_HEADS, HEAD_DIM, HEAD_DIM), F32)
    o, s_out = gdn_seq(proj, ab, conv_w, a_log, dt_bias, norm_w, past8, s0)
    x1 = matmul_res(o, w_out, x)
    return x1, s_out, proj[-(GDN_CONV - 1):, :nqkv]


def ffn_layer_seq(x, nw, wg, wu, cw, cb, wd, final_nw=None):
    act, st = ffn_up_seq(x, nw, wg, wu, cw, cb)
    return matmul_res(act, wd, x, final_nw=final_nw), st


def ffn_layer_step(x, nw, wg, wu, cw, cb, wd, past, final_nw=None):
    act, st = ffn_up_step(x, nw, wg, wu, cw, cb, past)
    return matmul_res(act, wd, x, final_nw=final_nw), st


KV_SECTION = NSA_GROUPS * HEAD_DIM
KV_ROTARY_SECTIONS = (2, 4)


def _rope_tables(pos):
    half = HEAD_DIM // 2
    inv = jnp.float32(ROPE_THETA) ** (-jnp.arange(half, dtype=F32) / half)
    ang = pos.astype(F32)[:, None] * inv[None, :]
    cos, sin = jnp.cos(ang), jnp.sin(ang)
    return jnp.concatenate([cos, cos], axis=1), jnp.concatenate([-sin, sin], axis=1)


def _rope(x, cosf, sinf):
    return x * cosf + pltpu.roll(x, HEAD_DIM // 2, 1) * sinf


def _kv_rows_kernel(x_ref, nw_ref, w_ref, cos_ref, sin_ref, o_ref, ob_ref, h_ref):
    j = pl.program_id(1)

    @pl.when(j == 0)
    def _():
        h_ref[...] = _bf(_rms(x_ref[...], nw_ref[...]))

    y = jnp.dot(h_ref[...], _bf(w_ref[...]), preferred_element_type=F32)
    is_rot = (j == KV_ROTARY_SECTIONS[0]) | (j == KV_ROTARY_SECTIONS[1])

    @pl.when(is_rot)
    def _():
        cosf, sinf = cos_ref[...], sin_ref[...]
        for g in range(NSA_GROUPS):
            sl = slice(g * HEAD_DIM, (g + 1) * HEAD_DIM)
            yg = _rope(y[:, sl], cosf, sinf)
            o_ref[:, sl] = yg
            ob_ref[:, sl] = _bf(yg)

    @pl.when(jnp.logical_not(is_rot))
    def _():
        o_ref[...] = y
        ob_ref[...] = _bf(y)


def kv_rows(x, nw, w, cosf, sinf, tm=1024):
    m, k = x.shape
    n = w.shape[1]
    tm = min(tm, m)
    return pl.pallas_call(
        _kv_rows_kernel,
        name="kv_rows",
        grid=(m // tm, n // KV_SECTION),
        in_specs=[
            pl.BlockSpec((tm, k), lambda i, j: (i, 0)),
            pl.BlockSpec((1, k), lambda i, j: (0, 0)),
            pl.BlockSpec((k, KV_SECTION), lambda i, j: (0, j)),
            pl.BlockSpec((tm, HEAD_DIM), lambda i, j: (i, 0)),
            pl.BlockSpec((tm, HEAD_DIM), lambda i, j: (i, 0)),
        ],
        out_specs=[pl.BlockSpec((tm, KV_SECTION), lambda i, j: (i, j)),
                   pl.BlockSpec((tm, KV_SECTION), lambda i, j: (i, j))],
        out_shape=[jax.ShapeDtypeStruct((m, n), F32), jax.ShapeDtypeStruct((m, n), BF16)],
        scratch_shapes=[pltpu.VMEM((tm, k), BF16)],
        compiler_params=_cparams("parallel", "arbitrary"),
    )(x, nw.reshape(1, k), w, cosf, sinf)


def _compress_seq_kernel(rows_ref, pe_ref, w1_ref, w2_ref, o_ref, xs_ref, ps_ref):
    nseg = o_ref.shape[2]
    half = CMP_STRIDE * HEAD_DIM
    ps_ref[...] = jnp.zeros(ps_ref.shape, F32)
    for t in range(CMP_STRIDE):
        sl = slice(t * HEAD_DIM, (t + 1) * HEAD_DIM)
        xs_ref[:, sl] = _bf(rows_ref[pl.ds(t, nseg, stride=CMP_STRIDE), :])
        ps_ref[0:1, sl] = pe_ref[0, t:t + 1, :]
        ps_ref[8:9, sl] = pe_ref[0, CMP_STRIDE + t:CMP_STRIDE + t + 1, :]
    xs = xs_ref[...]
    w_lo = _bf(w1_ref[0, 0:half, :])
    w_hi = _bf(w1_ref[0, half:2 * half, :])
    lo = jnp.dot(xs, w_lo, preferred_element_type=F32) + _dot(ps_ref[0:8, :], w_lo)[0:1]
    hi = jnp.dot(xs, w_hi, preferred_element_type=F32) + _dot(ps_ref[8:16, :], w_hi)[0:1]
    hid = _silu(lo + pltpu.roll(hi, nseg - 1, 0))
    o_ref[0, 0] = _bf(_dot(hid, w2_ref[0]))


def compress_seq(kv, pe, w1, w2):
    t = kv.shape[0]
    nseg = t // CMP_STRIDE
    hid = w1.shape[2]
    return pl.pallas_call(
        _compress_seq_kernel,
        name="compress_seq",
        grid=(2, NSA_GROUPS),
        in_specs=[
            pl.BlockSpec((t, HEAD_DIM), lambda a, g: (0, a * NSA_GROUPS + g)),
            pl.BlockSpec((1, CMP_BLOCK, HEAD_DIM), lambda a, g: (a, 0, 0)),
            pl.BlockSpec((1, CMP_BLOCK * HEAD_DIM, hid), lambda a, g: (a, 0, 0)),
            pl.BlockSpec((1, hid, HEAD_DIM), lambda a, g: (a, 0, 0)),
        ],
        out_specs=pl.BlockSpec((1, 1, nseg, HEAD_DIM), lambda a, g: (a, g, 0, 0)),
        out_shape=jax.ShapeDtypeStruct((2, NSA_GROUPS, nseg, HEAD_DIM), BF16),
        scratch_shapes=[pltpu.VMEM((nseg, CMP_STRIDE * HEAD_DIM), BF16),
                        pltpu.VMEM((16, CMP_STRIDE * HEAD_DIM), F32)],
        compiler_params=_cparams("arbitrary", "arbitrary"),
    )(kv, pe, w1, w2)


def _group_rows(ref, g):
    return jnp.concatenate([ref[:, (NSA_HPG * g + h) * HEAD_DIM:(NSA_HPG * g + h + 1) * HEAD_DIM]
                            for h in range(NSA_HPG)], axis=0)


def _masked_softmax(s, mask, exp=jnp.exp):
    s = jnp.where(mask, s, NEG_BIG)
    m = jnp.max(s, axis=1, keepdims=True)
    e = jnp.where(mask, exp(s - m), 0.0)
    return e / jnp.maximum(jnp.sum(e, axis=1, keepdims=True), 1e-30)


def _split3(x):
    hi = _bf(x)
    r = x - hi.astype(F32)
    mid = _bf(r)
    return hi, mid, _bf(r - mid.astype(F32))


def _select_blocks(score, blk, n_pick, axis=1):
    sel = jnp.zeros(score.shape, F32)
    for _ in range(n_pick):
        mx = jnp.max(score, axis=axis, keepdims=True)
        idx = jnp.min(jnp.where(score == mx, blk, 1e9), axis=axis, keepdims=True)
        pick = blk == idx
        sel = jnp.where(pick, 1.0, sel)
        score = jnp.where(pick, -3.0, score)
    return sel


def _select_blocks_ranked(score, n_rows, n_pick):
    n = score.shape[1]
    st = score.T
    ii = lax.broadcasted_iota(jnp.int32, (n, n), 0)
    jj = lax.broadcasted_iota(jnp.int32, (n, n), 1)
    rows = []
    for r in range(n_rows):
        mine = jnp.broadcast_to(score[r:r + 1, :], (n, n))
        other = jnp.broadcast_to(st[:, r:r + 1], (n, n))
        beats = jnp.where(other > mine, 1.0, jnp.where((other == mine) & (ii < jj), 1.0, 0.0))
        rank = jnp.sum(beats, axis=0, keepdims=True)
        rows.append(jnp.where(rank < n_pick, 1.0, 0.0))
    return _pad_rows(rows, score.shape[0])


def _nsa_cmp_kernel(q_ref, cos_ref, sin_ref, kc_ref, vc_ref, oc_ref, sel_ref, qrot_ref, *, n_sel):
    i = pl.program_id(0)
    Q = Q_BLOCK
    nc = kc_ref.shape[1]
    scale = HEAD_DIM ** -0.5
    cosf, sinf = cos_ref[...], sin_ref[...]
    for h in range(NSA_HEADS):
        sl = slice(h * HEAD_DIM, (h + 1) * HEAD_DIM)
        qrot_ref[:, sl] = _bf(_rope(q_ref[:, sl], cosf, sinf) * (scale * LOG2E))

    qpos_r = i * Q + lax.broadcasted_iota(jnp.int32, (NSA_HPG * Q, nc), 0) % Q
    c_end = lax.broadcasted_iota(jnp.int32, (NSA_HPG * Q, nc), 1) * CMP_STRIDE + (CMP_BLOCK - 1)
    cmask = c_end <= qpos_r
    ci = lax.broadcasted_iota(jnp.int32, (nc, 128), 0) * CMP_STRIDE
    sj = lax.broadcasted_iota(jnp.int32, (nc, 128), 1) * SEL_BLOCK
    cover = jnp.where((ci < sj + SEL_BLOCK) & (ci + CMP_BLOCK > sj), 1.0, 0.0).astype(BF16)
    qpos = i * Q + lax.broadcasted_iota(jnp.int32, (Q, 128), 0)
    blk = lax.broadcasted_iota(jnp.int32, (Q, 128), 1)
    valid = blk * SEL_BLOCK <= qpos
    force = (blk == 0) | (blk >= qpos // SEL_BLOCK - 1)

    scores = []
    for g in range(NSA_GROUPS):
        p = _masked_softmax(_dot_nt(_group_rows(q_ref, g), kc_ref[g]) * scale, cmask)
        oc = _dot(p, vc_ref[g])
        for h in range(NSA_HPG):
            oc_ref[:, (NSA_HPG * g + h) * HEAD_DIM:(NSA_HPG * g + h + 1) * HEAD_DIM] = oc[h * Q:(h + 1) * Q]
        ps = p[0:Q] + p[Q:2 * Q] + p[2 * Q:3 * Q] + p[3 * Q:4 * Q]
        imp = sum(jnp.dot(part, cover, preferred_element_type=F32) for part in _split3(ps))
        score = jnp.where(valid, jnp.where(force, 1e9, imp), -1.0)
        scores.append(jnp.where(blk < n_sel, score, -2.0))
    score_t = jnp.concatenate([s.T for s in scores], axis=1)
    blk_t = lax.broadcasted_iota(jnp.int32, score_t.shape, 0).astype(F32)
    picked_t = _select_blocks(score_t, blk_t, N_SELECT, axis=0)
    for g in range(NSA_GROUPS):
        picked = picked_t[:, g * Q:(g + 1) * Q].T
        sel_ref[:, g * 128:(g + 1) * 128] = _bf(jnp.where(picked > 0.5, 0.0, NEG_BIG))


def nsa_cmp(q, cosf, sinf, kc, vc):
    t, d = q.shape
    n_sel = t // SEL_BLOCK
    assert N_SELECT <= n_sel <= 128 and t % Q_BLOCK == 0
    nc = kc.shape[1]
    row = lambda w: pl.BlockSpec((Q_BLOCK, w), lambda i: (i, 0))
    full = pl.BlockSpec((NSA_GROUPS, nc, HEAD_DIM), lambda i: (0, 0, 0))
    return pl.pallas_call(
        functools.partial(_nsa_cmp_kernel, n_sel=n_sel),
        name="nsa_cmp",
        grid=(t // Q_BLOCK,),
        in_specs=[row(d), row(HEAD_DIM), row(HEAD_DIM), full, full],
        out_specs=[row(d), row(NSA_GROUPS * 128), row(d)],
        out_shape=[jax.ShapeDtypeStruct((t, d), F32), jax.ShapeDtypeStruct((t, NSA_GROUPS * 128), BF16),
                   jax.ShapeDtypeStruct((t, d), BF16)],
        compiler_params=_cparams("parallel"),
    )(q, cosf, sinf, kc, vc)


SEL_KEYS = 512


def _nsa_sel_kernel(qi_ref, kt_ref, q_ref, sel_ref, k_ref, v_ref, o_ref, m_scr, l_scr, acc_scr):
    n = pl.program_id(0)
    i, kt = qi_ref[n], kt_ref[n]
    Q = Q_BLOCK
    last = (i * Q + Q - 1) // SEL_KEYS

    @pl.when(kt == 0)
    def _():
        m_scr[...] = jnp.full(m_scr.shape, NEG_BIG, F32)
        l_scr[...] = jnp.zeros(l_scr.shape, F32)
        acc_scr[...] = jnp.zeros(acc_scr.shape, F32)

    def tile(diagonal):
        brow = lax.broadcasted_iota(jnp.int32, (128, SEL_KEYS), 0)
        kcol = lax.broadcasted_iota(jnp.int32, (128, SEL_KEYS), 1)
        expand = jnp.where(brow == kt * (SEL_KEYS // SEL_BLOCK) + kcol // SEL_BLOCK, 1.0, 0.0).astype(BF16)
        if diagonal:
            qpos = i * Q + lax.broadcasted_iota(jnp.int32, (Q, SEL_KEYS), 0)
            kpos = kt * SEL_KEYS + lax.broadcasted_iota(jnp.int32, (Q, SEL_KEYS), 1)
            causal = kpos <= qpos
        gs = range(NSA_GROUPS)
        sl = [slice(g * HEAD_DIM, (g + 1) * HEAD_DIM) for g in gs]
        bias = [jnp.dot(sel_ref[:, g * 128:(g + 1) * 128], expand, preferred_element_type=F32) for g in gs]
        if diagonal:
            bias = [jnp.where(causal, b, NEG_BIG) for b in bias]
        s = [_dot_nt(_group_rows(q_ref, g), k_ref[:, sl[g]]) for g in gs]
        s = [(s[g].reshape(NSA_HPG, Q, SEL_KEYS) + bias[g][None]).reshape(NSA_HPG * Q, SEL_KEYS) for g in gs]
        cols = [slice(c * 128, (c + 1) * 128) for c in range(SEL_KEYS // 128)]
        rep = lambda x: jnp.broadcast_to(x, (NSA_HPG * Q, 128))
        m_old = [m_scr[g] for g in gs]
        m_new = [jnp.maximum(m_old[g], rep(jnp.max(s[g], axis=1, keepdims=True))) for g in gs]
        e = [jnp.concatenate([jnp.exp2(s[g][:, c] - m_new[g]) for c in cols], axis=1) for g in gs]
        pv = [_dot(e[g], v_ref[:, sl[g]]) for g in gs]
        for g in gs:
            alpha = jnp.exp2(m_old[g] - m_new[g])
            l_scr[g] = alpha * l_scr[g] + rep(jnp.sum(e[g], axis=1, keepdims=True))
            acc_scr[g] = alpha * acc_scr[g] + pv[g]
            m_scr[g] = m_new[g]

    pl.when(kt < last)(functools.partial(tile, False))
    pl.when(kt == last)(functools.partial(tile, True))

    @pl.when(kt == last)
    def _():
        for g in range(NSA_GROUPS):
            o = acc_scr[g] / jnp.maximum(l_scr[g], 1e-30)
            for h in range(NSA_HPG):
                o_ref[:, (NSA_HPG * g + h) * HEAD_DIM:(NSA_HPG * g + h + 1) * HEAD_DIM] = o[h * Q:(h + 1) * Q]


def nsa_sel(qrot, sel, kvb):
    t, d = qrot.shape
    pairs = [(i, kt) for i in range(t // Q_BLOCK) for kt in range((i * Q_BLOCK + Q_BLOCK - 1) // SEL_KEYS + 1)]
    qi = jnp.asarray(np.array([p[0] for p in pairs], np.int32))
    kti = jnp.asarray(np.array([p[1] for p in pairs], np.int32))
    qblk = lambda w: pl.BlockSpec((Q_BLOCK, w), lambda n, qi, kti: (qi[n], 0))
    kblk = lambda sec: pl.BlockSpec((SEL_KEYS, KV_SECTION), lambda n, qi, kti: (kti[n], sec))
    return pl.pallas_call(
        _nsa_sel_kernel,
        name="nsa_sel",
        grid_spec=pltpu.PrefetchScalarGridSpec(
            num_scalar_prefetch=2,
            grid=(len(pairs),),
            in_specs=[qblk(d), qblk(NSA_GROUPS * 128), kblk(2), kblk(3)],
            out_specs=qblk(d),
            scratch_shapes=[pltpu.VMEM((NSA_GROUPS, NSA_HPG * Q_BLOCK, 128), F32),
                            pltpu.VMEM((NSA_GROUPS, NSA_HPG * Q_BLOCK, 128), F32),
                            pltpu.VMEM((NSA_GROUPS, NSA_HPG * Q_BLOCK, HEAD_DIM), F32)],
        ),
        out_shape=jax.ShapeDtypeStruct((t, d), F32),
        compiler_params=_cparams("arbitrary"),
    )(qi, kti, qrot, sel, kvb, kvb)


WIN_BLOCKS = WINDOW // Q_BLOCK + 1


def _nsa_win_kernel(q_ref, *refs):
    k_refs, v_refs = refs[:WIN_BLOCKS], refs[WIN_BLOCKS:2 * WIN_BLOCKS]
    oc_ref, os_ref, gate_ref, o_ref = refs[2 * WIN_BLOCKS:]
    i = pl.program_id(0)
    Q = Q_BLOCK
    span = WIN_BLOCKS * Q
    kw = jnp.concatenate([r[...] for r in k_refs], axis=0)
    vw = jnp.concatenate([r[...] for r in v_refs], axis=0)
    qpos = i * Q + lax.broadcasted_iota(jnp.int32, (Q, span), 0)
    kpos = (i - (WIN_BLOCKS - 1)) * Q + lax.broadcasted_iota(jnp.int32, (Q, span), 1)
    bias = jnp.where((kpos <= qpos) & (kpos > qpos - WINDOW) & (kpos >= 0), 0.0, NEG_BIG)
    gates = _sigmoid(gate_ref[...])
    for g in range(NSA_GROUPS):
        sl = slice(g * HEAD_DIM, (g + 1) * HEAD_DIM)
        s = _dot_nt(_group_rows(q_ref, g), kw[:, sl])
        s = (s.reshape(NSA_HPG, Q, span) + bias[None]).reshape(NSA_HPG * Q, span)
        e = jnp.exp2(s - jnp.max(s, axis=1, keepdims=True))
        ow = _dot(e, vw[:, sl]) / jnp.maximum(jnp.sum(e, axis=1, keepdims=True), 1e-30)
        for h in range(NSA_HPG):
            head = NSA_HPG * g + h
            hs = slice(head * HEAD_DIM, (head + 1) * HEAD_DIM)
            o = (gates[:, 3 * head:3 * head + 1] * oc_ref[:, hs] + gates[:, 3 * head + 1:3 * head + 2] * os_ref[:, hs]
                 + gates[:, 3 * head + 2:3 * head + 3] * ow[h * Q:(h + 1) * Q])
            o_ref[:, hs] = _bf(o)


def nsa_win(qrot, kvb, oc, osel, gates):
    t, d = qrot.shape
    row = lambda w: pl.BlockSpec((Q_BLOCK, w), lambda i: (i, 0))
    kblk = lambda sec, b: pl.BlockSpec(
        (Q_BLOCK, KV_SECTION), lambda i: (jnp.maximum(i - (WIN_BLOCKS - 1) + b, 0), sec))
    return pl.pallas_call(
        _nsa_win_kernel,
        name="nsa_win",
        grid=(t // Q_BLOCK,),
        in_specs=([row(d)] + [kblk(4, b) for b in range(WIN_BLOCKS)] + [kblk(5, b) for b in range(WIN_BLOCKS)]
                  + [row(d), row(d), row(128)]),
        out_specs=row(d),
        out_shape=jax.ShapeDtypeStruct((t, d), BF16),
        compiler_params=_cparams("parallel"),
    )(qrot, *([kvb] * (2 * WIN_BLOCKS)), oc, osel, gates)


def nsa_layer_seq(x, nmix, w_q, w_out, kvb, kc, vc, cosf, sinf):
    nq = NSA_HEADS * HEAD_DIM
    q = rms_matmul(x, nmix, w_q, nq, 512)
    gates = rms_matmul(x, nmix, _pad_cols(w_q[:, nq:], 128), 128, 128)
    oc, sel, qrot = nsa_cmp(q, cosf, sinf, kc, vc)
    osel = nsa_sel(qrot, sel, kvb)
    o = nsa_win(qrot, kvb, oc, osel, gates)
    return matmul_res(o, w_out, x)


def _head_column(row, offset, n_heads):
    h = lax.broadcasted_iota(jnp.int32, (n_heads, 128), 0)
    lane = lax.broadcasted_iota(jnp.int32, (n_heads, 128), 1)
    return jnp.sum(jnp.where(lane == h + offset, row, 0.0), axis=1, keepdims=True)


def _pad_rows(rows, n):
    return jnp.concatenate(rows + [jnp.zeros((n - len(rows), rows[0].shape[1]), F32)], axis=0)


def _gdn_step_kernel(proj_ref, ab_ref, past_ref, cw_ref, alog_ref, dtb_ref, nw_ref, s_ref, o_ref, sout_ref):
    H, D = GDN_HEADS, HEAD_DIM
    proj = proj_ref[0]
    cw = cw_ref[...]
    y = cw[GDN_CONV - 1] * proj[0:3 * H]
    for j in range(GDN_CONV - 1):
        y = y + cw[j] * past_ref[0, j]
    y = _silu(y)
    q = _l2norm(y[0:H]) * (D ** -0.5)
    k = _l2norm(y[H:2 * H])
    v = y[2 * H:3 * H]
    ab = ab_ref[0]
    g_row = -jnp.exp(alog_ref[...]) * _softplus(ab + dtb_ref[...])
    eg = jnp.exp(_head_column(g_row, 0, H))
    beta = _head_column(_sigmoid(ab), H, H)
    kb = k * beta
    w = kb * eg
    qe = q * eg
    rs = [_dot(_pad_rows([w[h:h + 1], qe[h:h + 1]], 8), s_ref[0, h]) for h in range(H)]
    ws = jnp.concatenate([r[0:1] for r in rs], axis=0)
    qs = jnp.concatenate([r[1:2] for r in rs], axis=0)
    v_new = v * beta - ws
    o = qs + jnp.sum(q * k, axis=1, keepdims=True) * v_new
    for h in range(H):
        outer = _dot_tn(_pad_rows([k[h:h + 1]], 8), _pad_rows([v_new[h:h + 1]], 8))
        sout_ref[0, h] = s_ref[0, h] * eg[h:h + 1] + outer
    o_ref[0] = _bf(_rms(o, nw_ref[...]) * _silu(proj[3 * H:4 * H]))


def gdn_step(proj, ab, conv_past, conv_w, a_log, dt_bias, norm_w, s0):
    b = proj.shape[0]
    H, D = GDN_HEADS, HEAD_DIM
    pad16 = lambda a: jnp.pad(a.reshape(1, H), ((0, 0), (0, 128 - H)))
    o, s_out = pl.pallas_call(
        _gdn_step_kernel,
        name="gdn_step",
        grid=(b,),
        in_specs=[pl.BlockSpec((1, 4 * H, D), lambda i: (i, 0, 0)),
                  pl.BlockSpec((1, 1, 128), lambda i: (i, 0, 0)),
                  pl.BlockSpec((1, GDN_CONV - 1, 3 * H, D), lambda i: (i, 0, 0, 0)),
                  pl.BlockSpec((GDN_CONV, 3 * H, D), lambda i: (0, 0, 0)),
                  pl.BlockSpec((1, 128), lambda i: (0, 0)),
                  pl.BlockSpec((1, 128), lambda i: (0, 0)),
                  pl.BlockSpec((1, D), lambda i: (0, 0)),
                  pl.BlockSpec((1, H, D, D), lambda i: (i, 0, 0, 0))],
        out_specs=[pl.BlockSpec((1, H, D), lambda i: (i, 0, 0)),
                   pl.BlockSpec((1, H, D, D), lambda i: (i, 0, 0, 0))],
        out_shape=[jax.ShapeDtypeStruct((b, H, D), BF16), jax.ShapeDtypeStruct((b, H, D, D), F32)],
        compiler_params=_cparams("parallel"),
    )(proj.reshape(b, 4 * H, D), ab.reshape(b, 1, 128), conv_past.reshape(b, GDN_CONV - 1, 3 * H, D),
      conv_w.reshape(GDN_CONV, 3 * H, D), pad16(a_log), pad16(dt_bias), norm_w.reshape(1, D), s0)
    return o.reshape(b, H * D), s_out


def gdn_layer_step(x, nmix, w_in, conv_w, a_log, dt_bias, norm_w, w_out, conv_past, s0):
    nqkv = 3 * GDN_HEADS * HEAD_DIM
    nz = GDN_HEADS * HEAD_DIM
    proj = rms_matmul(x, nmix, w_in, nqkv + nz, 512)
    ab = rms_matmul(x, nmix, _pad_cols(w_in[:, nqkv + nz:], 128), 128, 128)
    o, s_out = gdn_step(proj, ab, conv_past, conv_w, a_log, dt_bias, norm_w, s0)
    conv_new = jnp.concatenate([conv_past[:, 1:], proj[:, None, :nqkv]], axis=1)
    return matmul_res(o, w_out, x), s_out, conv_new


def _compress_pages_kernel(pt_ref, *refs, n_pages):
    page_refs = refs[:n_pages]
    pe_ref, w1_ref, w2_ref, o_ref, xs_ref, ps_ref = refs[n_pages:]
    G, D = NSA_GROUPS, HEAD_DIM
    seg_pp = page_refs[0].shape[0]
    nseg = n_pages * seg_pp
    half = CMP_STRIDE * D
    ps_ref[...] = jnp.zeros(ps_ref.shape, F32)
    for a in range(2):
        for t in range(CMP_STRIDE):
            sl = slice(t * D, (t + 1) * D)
            ps_ref[a, 0:1, sl] = pe_ref[a, t:t + 1, :]
            ps_ref[a, 8:9, sl] = pe_ref[a, CMP_STRIDE + t:CMP_STRIDE + t + 1, :]
    n_slices = 4
    t_per = CMP_STRIDE // n_slices
    lo = [_dot(ps_ref[a, 0:8, :], w1_ref[a, 0:half, :])[0:1] for a in range(2)]
    hi = [_dot(ps_ref[a, 8:16, :], w1_ref[a, half:2 * half, :])[0:1] for a in range(2)]
    for ts in range(n_slices):
        for p in range(n_pages):
            for t in range(ts * t_per, (ts + 1) * t_per):
                by_vec = jnp.swapaxes(page_refs[p][:, t], 0, 1)
                for a in range(2):
                    for g in range(G):
                        xs_ref[a, g * nseg + p * seg_pp:g * nseg + (p + 1) * seg_pp, t * D:(t + 1) * D] = (
                            by_vec[a * G + g])
        ks = slice(ts * t_per * D, (ts + 1) * t_per * D)
        for a in range(2):
            xs = _bf(xs_ref[a, :, ks])
            lo[a] = lo[a] + jnp.dot(xs, w1_ref[a, ks, :], preferred_element_type=F32)
            hi[a] = hi[a] + jnp.dot(xs, w1_ref[a, half + ts * t_per * D:half + (ts + 1) * t_per * D, :],
                                    preferred_element_type=F32)
    for a in range(2):
        hi_next = jnp.concatenate([pltpu.roll(hi[a][g * nseg:(g + 1) * nseg], nseg - 1, 0) for g in range(G)], axis=0)
        o_ref[0, a] = _bf(_dot(_silu(lo[a] + hi_next), w2_ref[a]))


def compress_pages(cache, page_table, pe, w1b, w2):
    b, n_pages = page_table.shape
    page = cache.shape[1]
    seg_pp = page // CMP_STRIDE
    nseg = n_pages * seg_pp
    hid = w1b.shape[2]
    cache = cache.reshape(cache.shape[0] * seg_pp, CMP_STRIDE, 2 * NSA_GROUPS, HEAD_DIM)

    def page_spec(p):
        return pl.BlockSpec((seg_pp, CMP_STRIDE, 2 * NSA_GROUPS, HEAD_DIM), lambda i, pt: (pt[i, p], 0, 0, 0))

    const = lambda shape: pl.BlockSpec(shape, lambda i, pt: (0,) * len(shape))
    return pl.pallas_call(
        functools.partial(_compress_pages_kernel, n_pages=n_pages),
        name="compress_pages",
        grid_spec=pltpu.PrefetchScalarGridSpec(
            num_scalar_prefetch=1,
            grid=(b,),
            in_specs=[page_spec(p) for p in range(n_pages)] + [
                const((2, CMP_BLOCK, HEAD_DIM)), const((2, CMP_BLOCK * HEAD_DIM, hid)), const((2, hid, HEAD_DIM))],
            out_specs=pl.BlockSpec((1, 2, NSA_GROUPS * nseg, HEAD_DIM), lambda i, pt: (i, 0, 0, 0)),
            scratch_shapes=[pltpu.VMEM((2, NSA_GROUPS * nseg, CMP_STRIDE * HEAD_DIM), F32),
                            pltpu.VMEM((2, 16, CMP_STRIDE * HEAD_DIM), F32)],
        ),
        out_shape=jax.ShapeDtypeStruct((b, 2, NSA_GROUPS * nseg, HEAD_DIM), BF16),
        compiler_params=_cparams("parallel"),
    )(page_table, *([cache] * n_pages), pe, w1b, w2)


def _head_rows_of_group(x, g):
    h = lax.broadcasted_iota(jnp.int32, x.shape, 0)
    return jnp.where(h // NSA_HPG == g, x, 0.0)


def _nsa_step_kernel(pt_ref, *refs, n_pages):
    page_refs = refs[:n_pages]
    q_ref, gate_ref, cos_ref, sin_ref, kvc_ref, new_ref, win_ref, o_ref, nwin_ref = refs[n_pages:]
    H, G, D = NSA_HEADS, NSA_GROUPS, HEAD_DIM
    page = page_refs[0].shape[0] // (2 * G)
    past = n_pages * page
    nseg = past // CMP_STRIDE
    n_sel = past // SEL_BLOCK + 1
    wb = win_ref.shape[0] // (2 * G)
    scale = D ** -0.5
    q_raw = q_ref[0]
    q_rot = _rope(q_raw, cos_ref[...], sin_ref[...])
    nwin_ref[0:(wb - 1) * 2 * G, :] = win_ref[2 * G:wb * 2 * G, :]
    nwin_ref[(wb - 1) * 2 * G:wb * 2 * G, :] = new_ref[4 * G:6 * G, :]

    def token_rows(ref, n, kv, g):
        return ref[pl.ds(kv * G + g, n, stride=2 * G), :]

    lane = lax.broadcasted_iota(jnp.int32, (8, 128), 1)
    c_ok = lax.broadcasted_iota(jnp.int32, (H, nseg), 1) * CMP_STRIDE + (CMP_BLOCK - 1) <= past
    ci = lax.broadcasted_iota(jnp.int32, (nseg, 128), 0) * CMP_STRIDE
    sj = lax.broadcasted_iota(jnp.int32, (nseg, 128), 1) * SEL_BLOCK
    cover = jnp.where((ci < sj + SEL_BLOCK) & (ci + CMP_BLOCK > sj), 1.0, 0.0).astype(BF16)
    valid = lane * SEL_BLOCK <= past
    force = (lane == 0) | (lane >= past // SEL_BLOCK - 1)
    brow = lax.broadcasted_iota(jnp.int32, (128, past), 0)
    kcol = lax.broadcasted_iota(jnp.int32, (128, past), 1)
    expand = jnp.where(brow == kcol // SEL_BLOCK, 1.0, 0.0).astype(BF16)
    all_keys = jnp.full((H, wb), True)

    gs = range(G)
    own = lambda xs: sum(_head_rows_of_group(xs[g], g) for g in gs)
    kw = [token_rows(nwin_ref, wb, 0, g) for g in gs]
    vw = [token_rows(nwin_ref, wb, 1, g) for g in gs]
    kg = [jnp.concatenate([_bf(token_rows(r, page, 0, g)) for r in page_refs], axis=0) for g in gs]
    vg = [jnp.concatenate([_bf(token_rows(r, page, 1, g)) for r in page_refs], axis=0) for g in gs]
    s_win = [_dot_nt(q_rot, kw[g]) * scale for g in gs]
    s_sel = [_dot_nt(q_rot, kg[g]) * scale for g in gs]
    s_new = [jnp.sum(q_rot * new_ref[2 * G + g:2 * G + g + 1, :], axis=1, keepdims=True) * scale for g in gs]
    s_cmp = [_dot_nt(q_raw, kvc_ref[0, 0, g * nseg:(g + 1) * nseg, :]) * scale for g in gs]
    p_cmp = [_masked_softmax(s_cmp[g], c_ok) for g in gs]
    p_win = [_masked_softmax(s_win[g], all_keys) for g in gs]
    o_cmp = own([_dot(p_cmp[g], kvc_ref[0, 1, g * nseg:(g + 1) * nseg, :]) for g in gs])
    o_win = own([_dot(p_win[g], vw[g]) for g in gs])
    ps_rows = [jnp.sum(_head_rows_of_group(p_cmp[g], g), axis=0, keepdims=True) for g in gs]
    imp = sum(jnp.dot(part, cover, preferred_element_type=F32) for part in _split3(_pad_rows(ps_rows, 8)))
    score = jnp.where(valid, jnp.where(force, 1e9, imp), -1.0)
    score = jnp.where(lane < n_sel, score, -2.0)
    sel = _select_blocks_ranked(score, G, N_SELECT)
    picked = jnp.dot(_bf(sel), expand, preferred_element_type=F32)
    o_parts = []
    for g in gs:
        kmask = jnp.broadcast_to(picked[g:g + 1], (H, past)) > 0.5
        new_ok = sel[g:g + 1, n_sel - 1:n_sel] > 0.5
        s = jnp.where(kmask, s_sel[g], NEG_BIG)
        sn = jnp.where(new_ok, s_new[g], NEG_BIG)
        m = jnp.maximum(jnp.max(s, axis=1, keepdims=True), sn)
        e = jnp.where(kmask, jnp.exp(s - m), 0.0)
        e_new = jnp.where(new_ok, jnp.exp(sn - m), 0.0)
        den = jnp.maximum(jnp.sum(e, axis=1, keepdims=True) + e_new, 1e-30)
        o_parts.append((_dot(e, vg[g]) + e_new * new_ref[3 * G + g:3 * G + g + 1, :]) / den)
    o_sel = own(o_parts)

    gates = _sigmoid(gate_ref[0])
    hh = lax.broadcasted_iota(jnp.int32, (H, 128), 0)
    ll = lax.broadcasted_iota(jnp.int32, (H, 128), 1)
    gcol = lambda c: jnp.sum(jnp.where(ll == 3 * hh + c, gates, 0.0), axis=1, keepdims=True)
    o_ref[0] = _bf(gcol(0) * o_cmp + gcol(1) * o_sel + gcol(2) * o_win)


def nsa_step(q, gates, cosf, sinf, kvc, kv_new, cache_sel, cache_win, page_table):
    b, n_pages = page_table.shape
    page = cache_sel.shape[1]
    wb = cache_win.shape[1]
    H, G, D = NSA_HEADS, NSA_GROUPS, HEAD_DIM

    rows_pp, rows_w = page * 2 * G, wb * 2 * G
    cache_sel = cache_sel.reshape(cache_sel.shape[0] * rows_pp, D)
    cache_win2 = cache_win.reshape(b * rows_w, D)

    def page_spec(p):
        return pl.BlockSpec((rows_pp, D), lambda i, pt: (pt[i, p], 0))

    per_seq = lambda *shape: pl.BlockSpec((1,) + shape, lambda i, pt: (i,) + (0,) * len(shape))
    flat_seq = lambda rows: pl.BlockSpec((rows, D), lambda i, pt: (i, 0))
    const = lambda *shape: pl.BlockSpec(shape, lambda i, pt: (0,) * len(shape))
    o, nwin = pl.pallas_call(
        functools.partial(_nsa_step_kernel, n_pages=n_pages),
        name="nsa_step",
        grid_spec=pltpu.PrefetchScalarGridSpec(
            num_scalar_prefetch=1,
            grid=(b,),
            in_specs=[page_spec(p) for p in range(n_pages)] + [
                per_seq(H, D), per_seq(1, 128), const(1, D), const(1, D),
                per_seq(2, kvc.shape[2], D), flat_seq(6 * G), flat_seq(rows_w)],
            out_specs=[per_seq(H, D), flat_seq(rows_w)],
        ),
        out_shape=[jax.ShapeDtypeStruct((b, H, D), BF16), jax.ShapeDtypeStruct((b * rows_w, D), F32)],
        compiler_params=_cparams("parallel"),
    )(page_table, *([cache_sel] * n_pages), q.reshape(b, H, D), gates.reshape(b, 1, 128), cosf, sinf,
      kvc, kv_new.reshape(b * 6 * G, D), cache_win2)
    return o.reshape(b, H * D), nwin.reshape(cache_win.shape)


def kernel(x_prompt, x_sample, state_gdn, state_gdn_conv, state_ffn_conv, cache_cmp_kv, cache_sel_kv, cache_win_kv, page_table, norm_mixer, norm_ffn, norm_kv, norm_final, gdn_w_in, gdn_conv_w, gdn_A_log, gdn_dt_bias, gdn_norm_w, gdn_w_out, nsa_w_q, nsa_w_out, kv_w, cmp_pe_k, cmp_pe_v, cmp_w1_k, cmp_w2_k, cmp_w1_v, cmp_w2_v, ffn_w_gate, ffn_w_up, ffn_conv_w, ffn_conv_b, ffn_w_down):
    cmp_pe = jnp.stack([cmp_pe_k, cmp_pe_v])
    cmp_w1 = jnp.stack([cmp_w1_k, cmp_w1_v])
    cmp_w2 = jnp.stack([cmp_w2_k, cmp_w2_v])

    x = x_prompt[0]
    t = x.shape[0]
    x, gdn_s_p, gdn_c_p = gdn_layer_seq(x, norm_mixer[0], gdn_w_in[0], gdn_conv_w[0], gdn_A_log[0], gdn_dt_bias[0],
                                        gdn_norm_w[0], gdn_w_out[0])
    x, ffn_c0_p = ffn_layer_seq(x, norm_ffn[0], (ffn_w_gate, 0), (ffn_w_up, 0), ffn_conv_w[0], ffn_conv_b[0],
                                (ffn_w_down, 0))
    cosf, sinf = _rope_tables(jnp.arange(t, dtype=jnp.int32))
    kv, kvb = kv_rows(x, norm_kv, kv_w, cosf, sinf)
    kvc = compress_seq(kv, cmp_pe, cmp_w1, cmp_w2)
    x = nsa_layer_seq(x, norm_mixer[1], nsa_w_q[0], nsa_w_out[0], kvb, kvc[0], kvc[1], cosf, sinf)
    y_p, ffn_c1_p = ffn_layer_seq(x, norm_ffn[1], (ffn_w_gate, 1), (ffn_w_up, 1), ffn_conv_w[1], ffn_conv_b[1],
                                  (ffn_w_down, 1), final_nw=norm_final)
    g, hd = NSA_GROUPS, HEAD_DIM
    nrow = 2 * g * hd
    cmp_p = kv[:, :nrow].reshape(1, t, 2, g, hd)
    sel_p = kv[:, nrow:2 * nrow].reshape(1, t, 2, g, hd)
    win_p = kv[t - min(WINDOW, t):, 2 * nrow:].reshape(1, min(WINDOW, t), 2, g, hd)
    assert x_sample.shape[1] == 1
    xs = x_sample[:, 0]
    b = xs.shape[0]
    n_pool, page = cache_sel_kv.shape[:2]
    past_len = page_table.shape[1] * page
    wb = cache_win_kv.shape[1]
    xs, gdn_s_s, gdn_c_s = gdn_layer_step(xs, norm_mixer[0], gdn_w_in[0], gdn_conv_w[0], gdn_A_log[0], gdn_dt_bias[0],
                                          gdn_norm_w[0], gdn_w_out[0], state_gdn_conv[0], state_gdn[0])
    xs, ffn_c0_s = ffn_layer_step(xs, norm_ffn[0], (ffn_w_gate, 0), (ffn_w_up, 0), ffn_conv_w[0], ffn_conv_b[0],
                                  (ffn_w_down, 0), state_ffn_conv[0])
    cos1, sin1 = _rope_tables(jnp.full((1,), past_len, jnp.int32))
    kv_s, _ = kv_rows(xs, norm_kv, kv_w, jnp.broadcast_to(cos1, (b, hd)), jnp.broadcast_to(sin1, (b, hd)))
    kvc_s = compress_pages(cache_cmp_kv, page_table, cmp_pe, _bf(cmp_w1), cmp_w2)
    nq = NSA_HEADS * hd
    q_s = rms_matmul(xs, norm_mixer[1], nsa_w_q[0], nq, 512)
    gates_s = rms_matmul(xs, norm_mixer[1], _pad_cols(nsa_w_q[0][:, nq:], 128), 128, 128)
    o_s, win_s = nsa_step(q_s, gates_s, cos1, sin1, kvc_s, kv_s, cache_sel_kv, cache_win_kv, page_table)
    xs = matmul_res(o_s, nsa_w_out[0], xs)
    y_s, ffn_c1_s = ffn_layer_step(xs, norm_ffn[1], (ffn_w_gate, 1), (ffn_w_up, 1), ffn_conv_w[1], ffn_conv_b[1],
                                   (ffn_w_down, 1), state_ffn_conv[1], final_nw=norm_final)
    cmp_s = kv_s[:, :nrow].reshape(b, 1, 2, g, hd)
    sel_s = kv_s[:, nrow:2 * nrow].reshape(b, 1, 2, g, hd)

    return (y_p[None], y_s[:, None],
            gdn_s_p[None, None], gdn_c_p[None, None], jnp.stack([ffn_c0_p, ffn_c1_p])[:, None], cmp_p, sel_p, win_p,
            gdn_s_s[None], gdn_c_s[None], jnp.stack([ffn_c0_s, ffn_c1_s]), cmp_s, sel_s, win_s)
```

```python
import functools

import jax
import jax.numpy as jnp
import numpy as np
from jax import lax
from jax.experimental import pallas as pl
from jax.experimental.pallas import tpu as pltpu

F32 = jnp.float32
BF16 = jnp.bfloat16

RMS_EPS = 1e-6
ROPE_THETA = 10000.0
HEAD_DIM = 128
GDN_HEADS = 16
GDN_CONV = 4
GDN_CHUNK = 128
NSA_HEADS = 16
NSA_GROUPS = 4
NSA_HPG = NSA_HEADS // NSA_GROUPS
CMP_STRIDE = 16
CMP_BLOCK = 32
SEL_BLOCK = 64
N_SELECT = 16
WINDOW = 512
Q_BLOCK = 128
FFN_CONV = 3
NEG_BIG = -1e30
LOG2E = 1.4426950408889634

VMEM_LIMIT = 52 * 1024 * 1024


def _cparams(*sem):
    return pltpu.CompilerParams(dimension_semantics=sem, vmem_limit_bytes=VMEM_LIMIT)


def _bf(x):
    return x.astype(BF16)


def _dot(a, b):
    return jnp.dot(_bf(a), _bf(b), preferred_element_type=F32)


def _dot_nt(a, b):
    return lax.dot_general(_bf(a), _bf(b), (((1,), (1,)), ((), ())), preferred_element_type=F32)


def _dot_tn(a, b):
    return lax.dot_general(_bf(a), _bf(b), (((0,), (0,)), ((), ())), preferred_element_type=F32)


def _sigmoid(x):
    return 1.0 / (1.0 + jnp.exp(-x))


def _silu(x):
    return x * _sigmoid(x)


def _rms(x, w):
    return x * lax.rsqrt(jnp.mean(x * x, axis=-1, keepdims=True) + RMS_EPS) * w


def _layered(w):
    return w if isinstance(w, tuple) else (w[None], 0)


def _w_spec(block, index_map, layer):
    return pl.BlockSpec((None,) + block, lambda *idx: (layer,) + index_map(*idx))


def _rms_mm_kernel(x_ref, nw_ref, w_ref, o_ref, h_ref):
    @pl.when(pl.program_id(1) == 0)
    def _():
        h_ref[...] = _bf(_rms(x_ref[...], nw_ref[...]))

    o_ref[...] = jnp.dot(h_ref[...], _bf(w_ref[...]), preferred_element_type=F32).astype(o_ref.dtype)


def rms_matmul(x, nw, w, n_out, tn, tm=2048, out_dtype=F32):
    m, k = x.shape
    wide = n_out // tn >= 4
    tm = min(tm if wide else tm // 2, m)
    return pl.pallas_call(
        _rms_mm_kernel,
        name="rms_mm",
        grid=(m // tm, n_out // tn),
        in_specs=[
            pl.BlockSpec((tm, k), lambda i, j: (i, 0), pipeline_mode=pl.Buffered(1 if wide else 2)),
            pl.BlockSpec((1, k), lambda i, j: (0, 0)),
            pl.BlockSpec((k, tn), lambda i, j: (0, j)),
        ],
        out_specs=pl.BlockSpec((tm, tn), lambda i, j: (i, j)),
        out_shape=jax.ShapeDtypeStruct((m, n_out), out_dtype),
        scratch_shapes=[pltpu.VMEM((tm, k), BF16)],
        compiler_params=_cparams("parallel", "arbitrary"),
    )(x, nw.reshape(1, k), w)


def _mm_res_kernel(a_ref, w_ref, r_ref, *rest, final_norm):
    if final_norm:
        nw_ref, o_ref = rest
    else:
        (o_ref,) = rest
    kk = pl.program_id(1)

    @pl.when(kk == 0)
    def _():
        o_ref[...] = r_ref[...]

    o_ref[...] += jnp.dot(a_ref[...], _bf(w_ref[...]), preferred_element_type=F32)

    if final_norm:
        @pl.when(kk == pl.num_programs(1) - 1)
        def _():
            o_ref[...] = _rms(o_ref[...], nw_ref[...])


def matmul_res(a, w, res, final_nw=None, tm=1024, tk=512):
    m, k = a.shape
    w, layer = _layered(w)
    n = w.shape[2]
    tm = min(tm, m)
    in_specs = [
        pl.BlockSpec((tm, tk), lambda i, kk: (i, kk)),
        _w_spec((tk, n), lambda i, kk: (kk, 0), layer),
        pl.BlockSpec((tm, n), lambda i, kk: (i, 0)),
    ]
    args = [a, w, res]
    if final_nw is not None:
        in_specs.append(pl.BlockSpec((1, n), lambda i, kk: (0, 0)))
        args.append(final_nw.reshape(1, n))
    return pl.pallas_call(
        functools.partial(_mm_res_kernel, final_norm=final_nw is not None),
        name="mm_res",
        grid=(m // tm, k // tk),
        in_specs=in_specs,
        out_specs=pl.BlockSpec((tm, n), lambda i, kk: (i, 0)),
        out_shape=jax.ShapeDtypeStruct((m, n), F32),
        compiler_params=_cparams("parallel", "arbitrary"),
    )(*args)


def _ffn_up_seq_kernel(x_ref, nw_ref, wg_ref, wu_ref, cw_ref, cb_ref, act_ref, st_ref, h_ref, carry_ref, buf_ref):
    i, j = pl.program_id(0), pl.program_id(1)
    tm = x_ref.shape[0]

    @pl.when(j == 0)
    def _():
        h_ref[...] = _bf(_rms(x_ref[...], nw_ref[...]))

    @pl.when(i == 0)
    def _():
        carry_ref[j] = jnp.zeros(carry_ref.shape[1:], F32)

    h = h_ref[...]
    g = jnp.dot(h, _bf(wg_ref[...]), preferred_element_type=F32)
    u = jnp.dot(h, _bf(wu_ref[...]), preferred_element_type=F32)
    buf_ref[0:8, :] = carry_ref[j]
    buf_ref[8:8 + tm, :] = g
    cw = cw_ref[...]
    a = cw[0:1] * buf_ref[6:6 + tm, :] + cw[1:2] * buf_ref[7:7 + tm, :] + cw[2:3] * g
    act_ref[...] = _bf(_silu(a + cb_ref[...]) * u)
    carry_ref[j] = g[tm - 8:tm]
    st_ref[0] = g[tm - 8:tm]


def ffn_up_seq(x, nw, wg, wu, cw, cb, tm=1024, tn=512):
    m, k = x.shape
    (wg, lg), (wu, lu) = _layered(wg), _layered(wu)
    f = wg.shape[2]
    nb = f // tn
    act, st = pl.pallas_call(
        _ffn_up_seq_kernel,
        name="ffn_up_seq",
        grid=(m // tm, nb),
        in_specs=[
            pl.BlockSpec((tm, k), lambda i, j: (i, 0)),
            pl.BlockSpec((1, k), lambda i, j: (0, 0)),
            _w_spec((k, tn), lambda i, j: (0, j), lg),
            _w_spec((k, tn), lambda i, j: (0, j), lu),
            pl.BlockSpec((FFN_CONV, tn), lambda i, j: (0, j)),
            pl.BlockSpec((1, tn), lambda i, j: (0, j)),
        ],
        out_specs=[
            pl.BlockSpec((tm, tn), lambda i, j: (i, j)),
            pl.BlockSpec((1, 8, tn), lambda i, j: (i, 0, j)),
        ],
        out_shape=[
            jax.ShapeDtypeStruct((m, f), BF16),
            jax.ShapeDtypeStruct((m // tm, 8, f), F32),
        ],
        scratch_shapes=[
            pltpu.VMEM((tm, k), BF16),
            pltpu.VMEM((nb, 8, tn), F32),
            pltpu.VMEM((tm + 8, tn), F32),
        ],
        compiler_params=_cparams("arbitrary", "arbitrary"),
    )(x, nw.reshape(1, k), wg, wu, cw, cb.reshape(1, f))
    return act, st[-1, 8 - (FFN_CONV - 1):]


def _ffn_up_step_kernel(x_ref, nw_ref, wg_ref, wu_ref, cw_ref, cb_ref, p0_ref, p1_ref, act_ref, g_ref, h_ref):
    @pl.when(pl.program_id(1) == 0)
    def _():
        h_ref[...] = _bf(_rms(x_ref[...], nw_ref[...]))

    h = h_ref[...]
    g = jnp.dot(h, _bf(wg_ref[...]), preferred_element_type=F32)
    u = jnp.dot(h, _bf(wu_ref[...]), preferred_element_type=F32)
    cw = cw_ref[...]
    a = cw[0:1] * p0_ref[...] + cw[1:2] * p1_ref[...] + cw[2:3] * g
    act_ref[...] = _bf(_silu(a + cb_ref[...]) * u)
    g_ref[...] = g


def ffn_up_step(x, nw, wg, wu, cw, cb, past, tn=512):
    m, k = x.shape
    (wg, lg), (wu, lu) = _layered(wg), _layered(wu)
    f = wg.shape[2]
    nb = f // tn
    past2 = past.reshape(m, (FFN_CONV - 1) * f)
    act, g = pl.pallas_call(
        _ffn_up_step_kernel,
        name="ffn_up_step",
        grid=(1, nb),
        in_specs=[
            pl.BlockSpec((m, k), lambda i, j: (0, 0)),
            pl.BlockSpec((1, k), lambda i, j: (0, 0)),
            _w_spec((k, tn), lambda i, j: (0, j), lg),
            _w_spec((k, tn), lambda i, j: (0, j), lu),
            pl.BlockSpec((FFN_CONV, tn), lambda i, j: (0, j)),
            pl.BlockSpec((1, tn), lambda i, j: (0, j)),
            pl.BlockSpec((m, tn), lambda i, j: (0, j)),
            pl.BlockSpec((m, tn), lambda i, j: (0, j + nb)),
        ],
        out_specs=[
            pl.BlockSpec((m, tn), lambda i, j: (0, j)),
            pl.BlockSpec((m, tn), lambda i, j: (0, j)),
        ],
        out_shape=[
            jax.ShapeDtypeStruct((m, f), BF16),
            jax.ShapeDtypeStruct((m, f), F32),
        ],
        scratch_shapes=[pltpu.VMEM((m, k), BF16)],
        compiler_params=_cparams("arbitrary", "arbitrary"),
    )(x, nw.reshape(1, k), wg, wu, cw, cb.reshape(1, f), past2, past2)
    return act, jnp.stack([past[:, 1], g], axis=1)


GDN_HPS = 16


def _cumsum_rows(x):
    row = lax.broadcasted_iota(jnp.int32, x.shape, 0)
    s = 1
    while s < x.shape[0]:
        x = x + jnp.where(row >= s, pltpu.roll(x, s, 0), 0.0)
        s *= 2
    return x


def _softplus(x):
    return jnp.maximum(x, 0.0) + jnp.log1p(jnp.exp(-jnp.abs(x)))


def _l2norm(x):
    return x * lax.rsqrt(jnp.sum(x * x, axis=-1, keepdims=True) + 1e-6)


def _gdn_seq_kernel(qp_ref, kp_ref, vp_ref, qh_ref, kh_ref, vh_ref, qpast_ref, kpast_ref, vpast_ref,
                    qw_ref, kw_ref, vw_ref, z_ref, ab_ref, alog_ref, dtb_ref, nw_ref, s0_ref,
                    o_ref, sout_ref, s_scr, buf_ref):
    hg, c = pl.program_id(0), pl.program_id(1)
    C = GDN_CHUNK
    D = HEAD_DIM

    @pl.when(c == 0)
    def _():
        s_scr[...] = s0_ref[...]

    def conv(p_ref, h_ref, past_ref, w_ref):
        buf_ref[0:8, :] = jnp.where(c == 0, past_ref[...], h_ref[...])
        buf_ref[8:8 + C, :] = p_ref[...]
        w = w_ref[...]
        y = (w[0:1] * buf_ref[5:5 + C, :] + w[1:2] * buf_ref[6:6 + C, :]
             + w[2:3] * buf_ref[7:7 + C, :] + w[3:4] * buf_ref[8:8 + C, :])
        return _silu(y)

    qc = conv(qp_ref, qh_ref, qpast_ref, qw_ref)
    kc = conv(kp_ref, kh_ref, kpast_ref, kw_ref)
    vc = conv(vp_ref, vh_ref, vpast_ref, vw_ref)

    ab = ab_ref[...]
    lane = lax.broadcasted_iota(jnp.int32, (C, 128), 1)
    g_all = -jnp.exp(alog_ref[...]) * _softplus(ab + dtb_ref[...])
    gc_all = _cumsum_rows(g_all)
    beta_all = _sigmoid(ab)

    row = lax.broadcasted_iota(jnp.int32, (C, C), 0)
    col = lax.broadcasted_iota(jnp.int32, (C, C), 1)
    eye = (row == col).astype(F32)
    nw = nw_ref[...]

    hs = range(GDN_HPS)
    sl = [slice(j * D, (j + 1) * D) for j in hs]
    gcol = [jnp.sum(jnp.where(lane == hg * GDN_HPS + j, gc_all, 0.0), axis=1, keepdims=True) for j in hs]
    bcol = [jnp.sum(jnp.where(lane == hg * GDN_HPS + j + GDN_HEADS, beta_all, 0.0), axis=1, keepdims=True)
            for j in hs]
    colb = [jnp.broadcast_to(gcol[j], (C, C)) for j in hs]
    dec = [jnp.exp(jnp.minimum(colb[j] - colb[j].T, 0.0)) for j in hs]
    q = [_l2norm(qc[:, sl[j]]) * (D ** -0.5) for j in hs]
    k = [_l2norm(kc[:, sl[j]]) for j in hs]
    kb = [k[j] * bcol[j] for j in hs]
    a_dec = [_dot_nt(kb[j], k[j]) * dec[j] for j in hs]
    qk = [jnp.where(row >= col, _dot_nt(q[j], k[j]) * dec[j], 0.0) for j in hs]
    first = ((row ^ col) == 1) & ((row & 1) == 1)
    tm = [eye - jnp.where(first, a_dec[j], 0.0) for j in hs]
    s = 2
    while s < C:
        msk = ((row // (2 * s)) == (col // (2 * s))) & ((row & s) != 0) & ((col & s) == 0)
        x = [_dot(jnp.where(msk, a_dec[j], 0.0), tm[j]) for j in hs]
        tm = [tm[j] - _dot(tm[j], x[j]) for j in hs]
        s *= 2
    egc = [jnp.exp(gcol[j]) for j in hs]
    uw = [_dot(tm[j], jnp.concatenate([vc[:, sl[j]] * bcol[j], kb[j] * egc[j]], axis=1)) for j in hs]
    r = [_dot(jnp.concatenate([uw[j][:, D:], q[j] * egc[j]], axis=0), s_scr[j]) for j in hs]
    v_new = [uw[j][:, :D] - r[j][:C] for j in hs]
    o = [r[j][C:] + _dot(qk[j], v_new[j]) for j in hs]
    for j in hs:
        g_last = colb[j][C - 1:C, :]
        kdec = k[j] * jnp.exp(g_last[:, 0:1] - gcol[j])
        s_scr[j] = s_scr[j] * jnp.exp(g_last) + _dot_tn(kdec, v_new[j])
        o_ref[:, sl[j]] = _bf(_rms(o[j], nw) * _silu(z_ref[:, sl[j]]))

    @pl.when(c == pl.num_programs(1) - 1)
    def _():
        sout_ref[...] = s_scr[...]


def gdn_seq(proj, ab, conv_w, a_log, dt_bias, norm_w, past8, s0):
    t = proj.shape[0]
    C, W = GDN_CHUNK, GDN_HPS * HEAD_DIM
    nq = GDN_HEADS * HEAD_DIM // W
    pad16 = lambda a: jnp.pad(a.reshape(1, GDN_HEADS), ((0, 0), (0, 128 - GDN_HEADS)))

    def rows(off):
        return pl.BlockSpec((C, W), lambda hg, c: (c, off + hg))

    def halo(off):
        return pl.BlockSpec((8, W), lambda hg, c: (jnp.maximum(c * (C // 8) - 1, 0), off + hg))

    def fixed(rws, off):
        return pl.BlockSpec((rws, W), lambda hg, c: (0, off + hg))

    o, s_out = pl.pallas_call(
        _gdn_seq_kernel,
        name="gdn_seq",
        grid=(nq, t // C),
        in_specs=[rows(0), rows(nq), rows(2 * nq), halo(0), halo(nq), halo(2 * nq),
                  fixed(8, 0), fixed(8, nq), fixed(8, 2 * nq),
                  fixed(GDN_CONV, 0), fixed(GDN_CONV, nq), fixed(GDN_CONV, 2 * nq),
                  rows(3 * nq),
                  pl.BlockSpec((C, 128), lambda hg, c: (c, 0)),
                  pl.BlockSpec((1, 128), lambda hg, c: (0, 0)),
                  pl.BlockSpec((1, 128), lambda hg, c: (0, 0)),
                  pl.BlockSpec((1, HEAD_DIM), lambda hg, c: (0, 0)),
                  pl.BlockSpec((GDN_HPS, HEAD_DIM, HEAD_DIM), lambda hg, c: (hg, 0, 0))],
        out_specs=[pl.BlockSpec((C, W), lambda hg, c: (c, hg)),
                   pl.BlockSpec((GDN_HPS, HEAD_DIM, HEAD_DIM), lambda hg, c: (hg, 0, 0))],
        out_shape=[jax.ShapeDtypeStruct((t, GDN_HEADS * HEAD_DIM), BF16),
                   jax.ShapeDtypeStruct((GDN_HEADS, HEAD_DIM, HEAD_DIM), F32)],
        scratch_shapes=[pltpu.VMEM((GDN_HPS, HEAD_DIM, HEAD_DIM), F32),
                        pltpu.VMEM((C + 8, W), F32)],
        compiler_params=_cparams("parallel", "arbitrary"),
    )(proj, proj, proj, proj, proj, proj, past8, past8, past8,
      conv_w, conv_w, conv_w, proj, ab, pad16(a_log), pad16(dt_bias), norm_w.reshape(1, HEAD_DIM), s0)
    return o, s_out


def _pad_cols(w, n):
    return jnp.pad(w, ((0, 0), (0, n - w.shape[1])))


def gdn_layer_seq(x, nmix, w_in, conv_w, a_log, dt_bias, norm_w, w_out):
    nqkv = 3 * GDN_HEADS * HEAD_DIM
    nz = GDN_HEADS * HEAD_DIM
    proj = rms_matmul(x, nmix, w_in, nqkv + nz, 512)
    ab = rms_matmul(x, nmix, _pad_cols(w_in[:, nqkv + nz:], 128), 128, 128)
    past8 = jnp.zeros((8, nqkv), F32)
    s0 = jnp.zeros((GDN_HEADS, HEAD_DIM, HEAD_DIM), F32)
    o, s_out = gdn_seq(proj, ab, conv_w, a_log, dt_bias, norm_w, past8, s0)
    x1 = matmul_res(o, w_out, x)
    return x1, s_out, proj[-(GDN_CONV - 1):, :nqkv]


def ffn_layer_seq(x, nw, wg, wu, cw, cb, wd, final_nw=None):
    act, st = ffn_up_seq(x, nw, wg, wu, cw, cb)
    return matmul_res(act, wd, x, final_nw=final_nw), st


def ffn_layer_step(x, nw, wg, wu, cw, cb, wd, past, final_nw=None):
    act, st = ffn_up_step(x, nw, wg, wu, cw, cb, past)
    return matmul_res(act, wd, x, final_nw=final_nw), st


KV_SECTION = NSA_GROUPS * HEAD_DIM
KV_ROTARY_SECTIONS = (2, 4)


def _rope_tables(pos):
    half = HEAD_DIM // 2
    inv = jnp.float32(ROPE_THETA) ** (-jnp.arange(half, dtype=F32) / half)
    ang = pos.astype(F32)[:, None] * inv[None, :]
    cos, sin = jnp.cos(ang), jnp.sin(ang)
    return jnp.concatenate([cos, cos], axis=1), jnp.concatenate([-sin, sin], axis=1)


def _rope(x, cosf, sinf):
    return x * cosf + pltpu.roll(x, HEAD_DIM // 2, 1) * sinf


def _kv_rows_kernel(x_ref, nw_ref, w_ref, cos_ref, sin_ref, o_ref, ob_ref, h_ref):
    j = pl.program_id(1)

    @pl.when(j == 0)
    def _():
        h_ref[...] = _bf(_rms(x_ref[...], nw_ref[...]))

    y = jnp.dot(h_ref[...], _bf(w_ref[...]), preferred_element_type=F32)
    is_rot = (j == KV_ROTARY_SECTIONS[0]) | (j == KV_ROTARY_SECTIONS[1])

    @pl.when(is_rot)
    def _():
        cosf, sinf = cos_ref[...], sin_ref[...]
        for g in range(NSA_GROUPS):
            sl = slice(g * HEAD_DIM, (g + 1) * HEAD_DIM)
            yg = _rope(y[:, sl], cosf, sinf)
            o_ref[:, sl] = yg
            ob_ref[:, sl] = _bf(yg)

    @pl.when(jnp.logical_not(is_rot))
    def _():
        o_ref[...] = y
        ob_ref[...] = _bf(y)


def kv_rows(x, nw, w, cosf, sinf, tm=1024):
    m, k = x.shape
    n = w.shape[1]
    tm = min(tm, m)
    return pl.pallas_call(
        _kv_rows_kernel,
        name="kv_rows",
        grid=(m // tm, n // KV_SECTION),
        in_specs=[
            pl.BlockSpec((tm, k), lambda i, j: (i, 0)),
            pl.BlockSpec((1, k), lambda i, j: (0, 0)),
            pl.BlockSpec((k, KV_SECTION), lambda i, j: (0, j)),
            pl.BlockSpec((tm, HEAD_DIM), lambda i, j: (i, 0)),
            pl.BlockSpec((tm, HEAD_DIM), lambda i, j: (i, 0)),
        ],
        out_specs=[pl.BlockSpec((tm, KV_SECTION), lambda i, j: (i, j)),
                   pl.BlockSpec((tm, KV_SECTION), lambda i, j: (i, j))],
        out_shape=[jax.ShapeDtypeStruct((m, n), F32), jax.ShapeDtypeStruct((m, n), BF16)],
        scratch_shapes=[pltpu.VMEM((tm, k), BF16)],
        compiler_params=_cparams("parallel", "arbitrary"),
    )(x, nw.reshape(1, k), w, cosf, sinf)


def _compress_seq_kernel(rows_ref, pe_ref, w1_ref, w2_ref, o_ref, xs_ref, ps_ref):
    nseg = o_ref.shape[2]
    half = CMP_STRIDE * HEAD_DIM
    ps_ref[...] = jnp.zeros(ps_ref.shape, F32)
    for t in range(CMP_STRIDE):
        sl = slice(t * HEAD_DIM, (t + 1) * HEAD_DIM)
        xs_ref[:, sl] = _bf(rows_ref[pl.ds(t, nseg, stride=CMP_STRIDE), :])
        ps_ref[0:1, sl] = pe_ref[0, t:t + 1, :]
        ps_ref[8:9, sl] = pe_ref[0, CMP_STRIDE + t:CMP_STRIDE + t + 1, :]
    xs = xs_ref[...]
    w_lo = _bf(w1_ref[0, 0:half, :])
    w_hi = _bf(w1_ref[0, half:2 * half, :])
    lo = jnp.dot(xs, w_lo, preferred_element_type=F32) + _dot(ps_ref[0:8, :], w_lo)[0:1]
    hi = jnp.dot(xs, w_hi, preferred_element_type=F32) + _dot(ps_ref[8:16, :], w_hi)[0:1]
    hid = _silu(lo + pltpu.roll(hi, nseg - 1, 0))
    o_ref[0, 0] = _bf(_dot(hid, w2_ref[0]))


def compress_seq(kv, pe, w1, w2):
    t = kv.shape[0]
    nseg = t // CMP_STRIDE
    hid = w1.shape[2]
    return pl.pallas_call(
        _compress_seq_kernel,
        name="compress_seq",
        grid=(2, NSA_GROUPS),
        in_specs=[
            pl.BlockSpec((t, HEAD_DIM), lambda a, g: (0, a * NSA_GROUPS + g)),
            pl.BlockSpec((1, CMP_BLOCK, HEAD_DIM), lambda a, g: (a, 0, 0)),
            pl.BlockSpec((1, CMP_BLOCK * HEAD_DIM, hid), lambda a, g: (a, 0, 0)),
            pl.BlockSpec((1, hid, HEAD_DIM), lambda a, g: (a, 0, 0)),
        ],
        out_specs=pl.BlockSpec((1, 1, nseg, HEAD_DIM), lambda a, g: (a, g, 0, 0)),
        out_shape=jax.ShapeDtypeStruct((2, NSA_GROUPS, nseg, HEAD_DIM), BF16),
        scratch_shapes=[pltpu.VMEM((nseg, CMP_STRIDE * HEAD_DIM), BF16),
                        pltpu.VMEM((16, CMP_STRIDE * HEAD_DIM), F32)],
        compiler_params=_cparams("arbitrary", "arbitrary"),
    )(kv, pe, w1, w2)


def _group_rows(ref, g):
    return jnp.concatenate([ref[:, (NSA_HPG * g + h) * HEAD_DIM:(NSA_HPG * g + h + 1) * HEAD_DIM]
                            for h in range(NSA_HPG)], axis=0)


def _masked_softmax(s, mask, exp=jnp.exp):
    s = jnp.where(mask, s, NEG_BIG)
    m = jnp.max(s, axis=1, keepdims=True)
    e = jnp.where(mask, exp(s - m), 0.0)
    return e / jnp.maximum(jnp.sum(e, axis=1, keepdims=True), 1e-30)


def _split3(x):
    hi = _bf(x)
    r = x - hi.astype(F32)
    mid = _bf(r)
    return hi, mid, _bf(r - mid.astype(F32))


def _select_blocks(score, blk, n_pick, axis=1):
    sel = jnp.zeros(score.shape, F32)
    for _ in range(n_pick):
        mx = jnp.max(score, axis=axis, keepdims=True)
        idx = jnp.min(jnp.where(score == mx, blk, 1e9), axis=axis, keepdims=True)
        pick = blk == idx
        sel = jnp.where(pick, 1.0, sel)
        score = jnp.where(pick, -3.0, score)
    return sel


def _select_blocks_ranked(score, n_rows, n_pick):
    n = score.shape[1]
    st = score.T
    ii = lax.broadcasted_iota(jnp.int32, (n, n), 0)
    jj = lax.broadcasted_iota(jnp.int32, (n, n), 1)
    rows = []
    for r in range(n_rows):
        mine = jnp.broadcast_to(score[r:r + 1, :], (n, n))
        other = jnp.broadcast_to(st[:, r:r + 1], (n, n))
        beats = jnp.where(other > mine, 1.0, jnp.where((other == mine) & (ii < jj), 1.0, 0.0))
        rank = jnp.sum(beats, axis=0, keepdims=True)
        rows.append(jnp.where(rank < n_pick, 1.0, 0.0))
    return _pad_rows(rows, score.shape[0])


def _nsa_cmp_kernel(q_ref, cos_ref, sin_ref, kc_ref, vc_ref, oc_ref, sel_ref, qrot_ref, *, n_sel):
    i = pl.program_id(0)
    Q = Q_BLOCK
    nc = kc_ref.shape[1]
    scale = HEAD_DIM ** -0.5
    cosf, sinf = cos_ref[...], sin_ref[...]
    for h in range(NSA_HEADS):
        sl = slice(h * HEAD_DIM, (h + 1) * HEAD_DIM)
        qrot_ref[:, sl] = _bf(_rope(q_ref[:, sl], cosf, sinf) * (scale * LOG2E))

    qpos_r = i * Q + lax.broadcasted_iota(jnp.int32, (NSA_HPG * Q, nc), 0) % Q
    c_end = lax.broadcasted_iota(jnp.int32, (NSA_HPG * Q, nc), 1) * CMP_STRIDE + (CMP_BLOCK - 1)
    cmask = c_end <= qpos_r
    ci = lax.broadcasted_iota(jnp.int32, (nc, 128), 0) * CMP_STRIDE
    sj = lax.broadcasted_iota(jnp.int32, (nc, 128), 1) * SEL_BLOCK
    cover = jnp.where((ci < sj + SEL_BLOCK) & (ci + CMP_BLOCK > sj), 1.0, 0.0).astype(BF16)
    qpos = i * Q + lax.broadcasted_iota(jnp.int32, (Q, 128), 0)
    blk = lax.broadcasted_iota(jnp.int32, (Q, 128), 1)
    valid = blk * SEL_BLOCK <= qpos
    force = (blk == 0) | (blk >= qpos // SEL_BLOCK - 1)

    scores = []
    for g in range(NSA_GROUPS):
        p = _masked_softmax(_dot_nt(_group_rows(q_ref, g), kc_ref[g]) * scale, cmask)
        oc = _dot(p, vc_ref[g])
        for h in range(NSA_HPG):
            oc_ref[:, (NSA_HPG * g + h) * HEAD_DIM:(NSA_HPG * g + h + 1) * HEAD_DIM] = oc[h * Q:(h + 1) * Q]
        ps = p[0:Q] + p[Q:2 * Q] + p[2 * Q:3 * Q] + p[3 * Q:4 * Q]
        imp = sum(jnp.dot(part, cover, preferred_element_type=F32) for part in _split3(ps))
        score = jnp.where(valid, jnp.where(force, 1e9, imp), -1.0)
        scores.append(jnp.where(blk < n_sel, score, -2.0))
    score_t = jnp.concatenate([s.T for s in scores], axis=1)
    blk_t = lax.broadcasted_iota(jnp.int32, score_t.shape, 0).astype(F32)
    picked_t = _select_blocks(score_t, blk_t, N_SELECT, axis=0)
    for g in range(NSA_GROUPS):
        picked = picked_t[:, g * Q:(g + 1) * Q].T
        sel_ref[:, g * 128:(g + 1) * 128] = _bf(jnp.where(picked > 0.5, 0.0, NEG_BIG))


def nsa_cmp(q, cosf, sinf, kc, vc):
    t, d = q.shape
    n_sel = t // SEL_BLOCK
    assert N_SELECT <= n_sel <= 128 and t % Q_BLOCK == 0
    nc = kc.shape[1]
    row = lambda w: pl.BlockSpec((Q_BLOCK, w), lambda i: (i, 0))
    full = pl.BlockSpec((NSA_GROUPS, nc, HEAD_DIM), lambda i: (0, 0, 0))
    return pl.pallas_call(
        functools.partial(_nsa_cmp_kernel, n_sel=n_sel),
        name="nsa_cmp",
        grid=(t // Q_BLOCK,),
        in_specs=[row(d), row(HEAD_DIM), row(HEAD_DIM), full, full],
        out_specs=[row(d), row(NSA_GROUPS * 128), row(d)],
        out_shape=[jax.ShapeDtypeStruct((t, d), F32), jax.ShapeDtypeStruct((t, NSA_GROUPS * 128), BF16),
                   jax.ShapeDtypeStruct((t, d), BF16)],
        compiler_params=_cparams("parallel"),
    )(q, cosf, sinf, kc, vc)


SEL_KEYS = 512


def _nsa_sel_kernel(qi_ref, kt_ref, q_ref, sel_ref, k_ref, v_ref, o_ref, m_scr, l_scr, acc_scr):
    n = pl.program_id(0)
    i, kt = qi_ref[n], kt_ref[n]
    Q = Q_BLOCK
    last = (i * Q + Q - 1) // SEL_KEYS

    @pl.when(kt == 0)
    def _():
        m_scr[...] = jnp.full(m_scr.shape, NEG_BIG, F32)
        l_scr[...] = jnp.zeros(l_scr.shape, F32)
        acc_scr[...] = jnp.zeros(acc_scr.shape, F32)

    def tile(diagonal):
        brow = lax.broadcasted_iota(jnp.int32, (128, SEL_KEYS), 0)
        kcol = lax.broadcasted_iota(jnp.int32, (128, SEL_KEYS), 1)
        expand = jnp.where(brow == kt * (SEL_KEYS // SEL_BLOCK) + kcol // SEL_BLOCK, 1.0, 0.0).astype(BF16)
        if diagonal:
            qpos = i * Q + lax.broadcasted_iota(jnp.int32, (Q, SEL_KEYS), 0)
            kpos = kt * SEL_KEYS + lax.broadcasted_iota(jnp.int32, (Q, SEL_KEYS), 1)
            causal = kpos <= qpos
        gs = range(NSA_GROUPS)
        sl = [slice(g * HEAD_DIM, (g + 1) * HEAD_DIM) for g in gs]
        bias = [jnp.dot(sel_ref[:, g * 128:(g + 1) * 128], expand, preferred_element_type=F32) for g in gs]
        if diagonal:
            bias = [jnp.where(causal, b, NEG_BIG) for b in bias]
        s = [_dot_nt(_group_rows(q_ref, g), k_ref[:, sl[g]]) for g in gs]
        s = [(s[g].reshape(NSA_HPG, Q, SEL_KEYS) + bias[g][None]).reshape(NSA_HPG * Q, SEL_KEYS) for g in gs]
        cols = [slice(c * 128, (c + 1) * 128) for c in range(SEL_KEYS // 128)]
        rep = lambda x: jnp.broadcast_to(x, (NSA_HPG * Q, 128))
        m_old = [m_scr[g] for g in gs]
        m_new = [jnp.maximum(m_old[g], rep(jnp.max(s[g], axis=1, keepdims=True))) for g in gs]
        e = [jnp.concatenate([jnp.exp2(s[g][:, c] - m_new[g]) for c in cols], axis=1) for g in gs]
        pv = [_dot(e[g], v_ref[:, sl[g]]) for g in gs]
        for g in gs:
            alpha = jnp.exp2(m_old[g] - m_new[g])
            l_scr[g] = alpha * l_scr[g] + rep(jnp.sum(e[g], axis=1, keepdims=True))
            acc_scr[g] = alpha * acc_scr[g] + pv[g]
            m_scr[g] = m_new[g]

    pl.when(kt < last)(functools.partial(tile, False))
    pl.when(kt == last)(functools.partial(tile, True))

    @pl.when(kt == last)
    def _():
        for g in range(NSA_GROUPS):
            o = acc_scr[g] / jnp.maximum(l_scr[g], 1e-30)
            for h in range(NSA_HPG):
                o_ref[:, (NSA_HPG * g + h) * HEAD_DIM:(NSA_HPG * g + h + 1) * HEAD_DIM] = o[h * Q:(h + 1) * Q]


def nsa_sel(qrot, sel, kvb):
    t, d = qrot.shape
    pairs = [(i, kt) for i in range(t // Q_BLOCK) for kt in range((i * Q_BLOCK + Q_BLOCK - 1) // SEL_KEYS + 1)]
    qi = jnp.asarray(np.array([p[0] for p in pairs], np.int32))
    kti = jnp.asarray(np.array([p[1] for p in pairs], np.int32))
    qblk = lambda w: pl.BlockSpec((Q_BLOCK, w), lambda n, qi, kti: (qi[n], 0))
    kblk = lambda sec: pl.BlockSpec((SEL_KEYS, KV_SECTION), lambda n, qi, kti: (kti[n], sec))
    return pl.pallas_call(
        _nsa_sel_kernel,
        name="nsa_sel",
        grid_spec=pltpu.PrefetchScalarGridSpec(
            num_scalar_prefetch=2,
            grid=(len(pairs),),
            in_specs=[qblk(d), qblk(NSA_GROUPS * 128), kblk(2), kblk(3)],
            out_specs=qblk(d),
            scratch_shapes=[pltpu.VMEM((NSA_GROUPS, NSA_HPG * Q_BLOCK, 128), F32),
                            pltpu.VMEM((NSA_GROUPS, NSA_HPG * Q_BLOCK, 128), F32),
                            pltpu.VMEM((NSA_GROUPS, NSA_HPG * Q_BLOCK, HEAD_DIM), F32)],
        ),
        out_shape=jax.ShapeDtypeStruct((t, d), F32),
        compiler_params=_cparams("arbitrary"),
    )(qi, kti, qrot, sel, kvb, kvb)


WIN_BLOCKS = WINDOW // Q_BLOCK + 1


def _nsa_win_kernel(q_ref, *refs):
    k_refs, v_refs = refs[:WIN_BLOCKS], refs[WIN_BLOCKS:2 * WIN_BLOCKS]
    oc_ref, os_ref, gate_ref, o_ref = refs[2 * WIN_BLOCKS:]
    i = pl.program_id(0)
    Q = Q_BLOCK
    span = WIN_BLOCKS * Q
    kw = jnp.concatenate([r[...] for r in k_refs], axis=0)
    vw = jnp.concatenate([r[...] for r in v_refs], axis=0)
    qpos = i * Q + lax.broadcasted_iota(jnp.int32, (Q, span), 0)
    kpos = (i - (WIN_BLOCKS - 1)) * Q + lax.broadcasted_iota(jnp.int32, (Q, span), 1)
    bias = jnp.where((kpos <= qpos) & (kpos > qpos - WINDOW) & (kpos >= 0), 0.0, NEG_BIG)
    gates = _sigmoid(gate_ref[...])
    for g in range(NSA_GROUPS):
        sl = slice(g * HEAD_DIM, (g + 1) * HEAD_DIM)
        s = _dot_nt(_group_rows(q_ref, g), kw[:, sl])
        s = (s.reshape(NSA_HPG, Q, span) + bias[None]).reshape(NSA_HPG * Q, span)
        e = jnp.exp2(s - jnp.max(s, axis=1, keepdims=True))
        ow = _dot(e, vw[:, sl]) / jnp.maximum(jnp.sum(e, axis=1, keepdims=True), 1e-30)
        for h in range(NSA_HPG):
            head = NSA_HPG * g + h
            hs = slice(head * HEAD_DIM, (head + 1) * HEAD_DIM)
            o = (gates[:, 3 * head:3 * head + 1] * oc_ref[:, hs] + gates[:, 3 * head + 1:3 * head + 2] * os_ref[:, hs]
                 + gates[:, 3 * head + 2:3 * head + 3] * ow[h * Q:(h + 1) * Q])
            o_ref[:, hs] = _bf(o)


def nsa_win(qrot, kvb, oc, osel, gates):
    t, d = qrot.shape
    row = lambda w: pl.BlockSpec((Q_BLOCK, w), lambda i: (i, 0))
    kblk = lambda sec, b: pl.BlockSpec(
        (Q_BLOCK, KV_SECTION), lambda i: (jnp.maximum(i - (WIN_BLOCKS - 1) + b, 0), sec))
    return pl.pallas_call(
        _nsa_win_kernel,
        name="nsa_win",
        grid=(t // Q_BLOCK,),
        in_specs=([row(d)] + [kblk(4, b) for b in range(WIN_BLOCKS)] + [kblk(5, b) for b in range(WIN_BLOCKS)]
                  + [row(d), row(d), row(128)]),
        out_specs=row(d),
        out_shape=jax.ShapeDtypeStruct((t, d), BF16),
        compiler_params=_cparams("parallel"),
    )(qrot, *([kvb] * (2 * WIN_BLOCKS)), oc, osel, gates)


def nsa_layer_seq(x, nmix, w_q, w_out, kvb, kc, vc, cosf, sinf):
    nq = NSA_HEADS * HEAD_DIM
    q = rms_matmul(x, nmix, w_q, nq, 512)
    gates = rms_matmul(x, nmix, _pad_cols(w_q[:, nq:], 128), 128, 128)
    oc, sel, qrot = nsa_cmp(q, cosf, sinf, kc, vc)
    osel = nsa_sel(qrot, sel, kvb)
    o = nsa_win(qrot, kvb, oc, osel, gates)
    return matmul_res(o, w_out, x)


def _head_column(row, offset, n_heads):
    h = lax.broadcasted_iota(jnp.int32, (n_heads, 128), 0)
    lane = lax.broadcasted_iota(jnp.int32, (n_heads, 128), 1)
    return jnp.sum(jnp.where(lane == h + offset, row, 0.0), axis=1, keepdims=True)


def _pad_rows(rows, n):
    return jnp.concatenate(rows + [jnp.zeros((n - len(rows), rows[0].shape[1]), F32)], axis=0)


GDN_STEP_SEQS = 4


def _gdn_step_kernel(proj_ref, ab_ref, past_ref, cw_ref, alog_ref, dtb_ref, nw_ref, s_ref, o_ref, sout_ref):
    H, D = GDN_HEADS, HEAD_DIM
    cw = cw_ref[...]
    for sq in range(proj_ref.shape[0]):
        proj = proj_ref[sq]
        y = cw[GDN_CONV - 1] * proj[0:3 * H]
        for j in range(GDN_CONV - 1):
            y = y + cw[j] * past_ref[sq, j]
        y = _silu(y)
        q = _l2norm(y[0:H]) * (D ** -0.5)
        k = _l2norm(y[H:2 * H])
        v = y[2 * H:3 * H]
        ab = ab_ref[sq]
        g_row = -jnp.exp(alog_ref[...]) * _softplus(ab + dtb_ref[...])
        eg = jnp.exp(_head_column(g_row, 0, H))
        beta = _head_column(_sigmoid(ab), H, H)
        kb = k * beta
        w = kb * eg
        qe = q * eg
        rs = [_dot(_pad_rows([w[h:h + 1], qe[h:h + 1]], 8), s_ref[sq, h]) for h in range(H)]
        ws = jnp.concatenate([r[0:1] for r in rs], axis=0)
        qs = jnp.concatenate([r[1:2] for r in rs], axis=0)
        v_new = v * beta - ws
        o = qs + jnp.sum(q * k, axis=1, keepdims=True) * v_new
        for h in range(H):
            outer = _dot_tn(_pad_rows([k[h:h + 1]], 8), _pad_rows([v_new[h:h + 1]], 8))
            sout_ref[sq, h] = s_ref[sq, h] * eg[h:h + 1] + outer
        o_ref[sq] = _bf(_rms(o, nw_ref[...]) * _silu(proj[3 * H:4 * H]))


def gdn_step(proj, ab, conv_past, conv_w, a_log, dt_bias, norm_w, s0):
    b = proj.shape[0]
    H, D = GDN_HEADS, HEAD_DIM
    nb = GDN_STEP_SEQS if b % GDN_STEP_SEQS == 0 else 1
    pad16 = lambda a: jnp.pad(a.reshape(1, H), ((0, 0), (0, 128 - H)))
    o, s_out = pl.pallas_call(
        _gdn_step_kernel,
        name="gdn_step",
        grid=(b // nb,),
        in_specs=[pl.BlockSpec((nb, 4 * H, D), lambda i: (i, 0, 0)),
                  pl.BlockSpec((nb, 1, 128), lambda i: (i, 0, 0)),
                  pl.BlockSpec((nb, GDN_CONV - 1, 3 * H, D), lambda i: (i, 0, 0, 0)),
                  pl.BlockSpec((GDN_CONV, 3 * H, D), lambda i: (0, 0, 0)),
                  pl.BlockSpec((1, 128), lambda i: (0, 0)),
                  pl.BlockSpec((1, 128), lambda i: (0, 0)),
                  pl.BlockSpec((1, D), lambda i: (0, 0)),
                  pl.BlockSpec((nb, H, D, D), lambda i: (i, 0, 0, 0))],
        out_specs=[pl.BlockSpec((nb, H, D), lambda i: (i, 0, 0)),
                   pl.BlockSpec((nb, H, D, D), lambda i: (i, 0, 0, 0))],
        out_shape=[jax.ShapeDtypeStruct((b, H, D), BF16), jax.ShapeDtypeStruct((b, H, D, D), F32)],
        compiler_params=_cparams("parallel"),
    )(proj.reshape(b, 4 * H, D), ab.reshape(b, 1, 128), conv_past.reshape(b, GDN_CONV - 1, 3 * H, D),
      conv_w.reshape(GDN_CONV, 3 * H, D), pad16(a_log), pad16(dt_bias), norm_w.reshape(1, D), s0)
    return o.reshape(b, H * D), s_out


def gdn_layer_step(x, nmix, w_in, conv_w, a_log, dt_bias, norm_w, w_out, conv_past, s0):
    nqkv = 3 * GDN_HEADS * HEAD_DIM
    nz = GDN_HEADS * HEAD_DIM
    proj = rms_matmul(x, nmix, w_in, nqkv + nz, 512)
    ab = rms_matmul(x, nmix, _pad_cols(w_in[:, nqkv + nz:], 128), 128, 128)
    o, s_out = gdn_step(proj, ab, conv_past, conv_w, a_log, dt_bias, norm_w, s0)
    conv_new = jnp.concatenate([conv_past[:, 1:], proj[:, None, :nqkv]], axis=1)
    return matmul_res(o, w_out, x), s_out, conv_new


def _compress_pages_kernel(pt_ref, *refs, n_pages):
    page_refs = refs[:n_pages]
    pe_ref, w1_ref, w2_ref, o_ref, xs_ref, ps_ref = refs[n_pages:]
    G, D = NSA_GROUPS, HEAD_DIM
    seg_pp = page_refs[0].shape[0]
    nseg = n_pages * seg_pp
    half = CMP_STRIDE * D
    ps_ref[...] = jnp.zeros(ps_ref.shape, F32)
    for a in range(2):
        for t in range(CMP_STRIDE):
            sl = slice(t * D, (t + 1) * D)
            ps_ref[a, 0:1, sl] = pe_ref[a, t:t + 1, :]
            ps_ref[a, 8:9, sl] = pe_ref[a, CMP_STRIDE + t:CMP_STRIDE + t + 1, :]
    n_slices = 4
    t_per = CMP_STRIDE // n_slices
    lo = [_dot(ps_ref[a, 0:8, :], w1_ref[a, 0:half, :])[0:1] for a in range(2)]
    hi = [_dot(ps_ref[a, 8:16, :], w1_ref[a, half:2 * half, :])[0:1] for a in range(2)]
    for ts in range(n_slices):
        for p in range(n_pages):
            for t in range(ts * t_per, (ts + 1) * t_per):
                by_vec = jnp.swapaxes(page_refs[p][:, t], 0, 1)
                for a in range(2):
                    for g in range(G):
                        xs_ref[a, g * nseg + p * seg_pp:g * nseg + (p + 1) * seg_pp, t * D:(t + 1) * D] = (
                            by_vec[a * G + g])
        ks = slice(ts * t_per * D, (ts + 1) * t_per * D)
        for a in range(2):
            xs = _bf(xs_ref[a, :, ks])
            lo[a] = lo[a] + jnp.dot(xs, w1_ref[a, ks, :], preferred_element_type=F32)
            hi[a] = hi[a] + jnp.dot(xs, w1_ref[a, half + ts * t_per * D:half + (ts + 1) * t_per * D, :],
                                    preferred_element_type=F32)
    for a in range(2):
        hi_next = jnp.concatenate([pltpu.roll(hi[a][g * nseg:(g + 1) * nseg], nseg - 1, 0) for g in range(G)], axis=0)
        o_ref[0, a] = _bf(_dot(_silu(lo[a] + hi_next), w2_ref[a]))


def compress_pages(cache, page_table, pe, w1b, w2):
    b, n_pages = page_table.shape
    page = cache.shape[1]
    seg_pp = page // CMP_STRIDE
    nseg = n_pages * seg_pp
    hid = w1b.shape[2]
    cache = cache.reshape(cache.shape[0] * seg_pp, CMP_STRIDE, 2 * NSA_GROUPS, HEAD_DIM)

    def page_spec(p):
        return pl.BlockSpec((seg_pp, CMP_STRIDE, 2 * NSA_GROUPS, HEAD_DIM), lambda i, pt: (pt[i, p], 0, 0, 0))

    const = lambda shape: pl.BlockSpec(shape, lambda i, pt: (0,) * len(shape))
    return pl.pallas_call(
        functools.partial(_compress_pages_kernel, n_pages=n_pages),
        name="compress_pages",
        grid_spec=pltpu.PrefetchScalarGridSpec(
            num_scalar_prefetch=1,
            grid=(b,),
            in_specs=[page_spec(p) for p in range(n_pages)] + [
                const((2, CMP_BLOCK, HEAD_DIM)), const((2, CMP_BLOCK * HEAD_DIM, hid)), const((2, hid, HEAD_DIM))],
            out_specs=pl.BlockSpec((1, 2, NSA_GROUPS * nseg, HEAD_DIM), lambda i, pt: (i, 0, 0, 0)),
            scratch_shapes=[pltpu.VMEM((2, NSA_GROUPS * nseg, CMP_STRIDE * HEAD_DIM), F32),
                            pltpu.VMEM((2, 16, CMP_STRIDE * HEAD_DIM), F32)],
        ),
        out_shape=jax.ShapeDtypeStruct((b, 2, NSA_GROUPS * nseg, HEAD_DIM), BF16),
        compiler_params=_cparams("parallel"),
    )(page_table, *([cache] * n_pages), pe, w1b, w2)


def _head_rows_of_group(x, g):
    h = lax.broadcasted_iota(jnp.int32, x.shape, 0)
    return jnp.where(h // NSA_HPG == g, x, 0.0)


def _nsa_step_kernel(pt_ref, *refs, n_pages):
    page_refs = refs[:n_pages]
    q_ref, gate_ref, cos_ref, sin_ref, kvc_ref, new_ref, win_ref, o_ref, nwin_ref = refs[n_pages:]
    H, G, D = NSA_HEADS, NSA_GROUPS, HEAD_DIM
    page = page_refs[0].shape[0] // (2 * G)
    past = n_pages * page
    nseg = past // CMP_STRIDE
    n_sel = past // SEL_BLOCK + 1
    wb = win_ref.shape[0] // (2 * G)
    scale = D ** -0.5
    q_raw = q_ref[0]
    q_rot = _rope(q_raw, cos_ref[...], sin_ref[...])
    nwin_ref[0:(wb - 1) * 2 * G, :] = win_ref[2 * G:wb * 2 * G, :]
    nwin_ref[(wb - 1) * 2 * G:wb * 2 * G, :] = new_ref[4 * G:6 * G, :]

    def token_rows(ref, n, kv, g):
        return ref[pl.ds(kv * G + g, n, stride=2 * G), :]

    lane = lax.broadcasted_iota(jnp.int32, (8, 128), 1)
    c_ok = lax.broadcasted_iota(jnp.int32, (H, nseg), 1) * CMP_STRIDE + (CMP_BLOCK - 1) <= past
    ci = lax.broadcasted_iota(jnp.int32, (nseg, 128), 0) * CMP_STRIDE
    sj = lax.broadcasted_iota(jnp.int32, (nseg, 128), 1) * SEL_BLOCK
    cover = jnp.where((ci < sj + SEL_BLOCK) & (ci + CMP_BLOCK > sj), 1.0, 0.0).astype(BF16)
    valid = lane * SEL_BLOCK <= past
    force = (lane == 0) | (lane >= past // SEL_BLOCK - 1)
    brow = lax.broadcasted_iota(jnp.int32, (128, past), 0)
    kcol = lax.broadcasted_iota(jnp.int32, (128, past), 1)
    expand = jnp.where(brow == kcol // SEL_BLOCK, 1.0, 0.0).astype(BF16)
    all_keys = jnp.full((H, wb), True)

    gs = range(G)
    own = lambda xs: sum(_head_rows_of_group(xs[g], g) for g in gs)
    kw = [token_rows(nwin_ref, wb, 0, g) for g in gs]
    vw = [token_rows(nwin_ref, wb, 1, g) for g in gs]
    kg = [jnp.concatenate([_bf(token_rows(r, page, 0, g)) for r in page_refs], axis=0) for g in gs]
    vg = [jnp.concatenate([_bf(token_rows(r, page, 1, g)) for r in page_refs], axis=0) for g in gs]
    s_win = [_dot_nt(q_rot, kw[g]) * scale for g in gs]
    s_sel = [_dot_nt(q_rot, kg[g]) * scale for g in gs]
    s_new = [jnp.sum(q_rot * new_ref[2 * G + g:2 * G + g + 1, :], axis=1, keepdims=True) * scale for g in gs]
    s_cmp = [_dot_nt(q_raw, kvc_ref[0, 0, g * nseg:(g + 1) * nseg, :]) * scale for g in gs]
    p_cmp = [_masked_softmax(s_cmp[g], c_ok) for g in gs]
    p_win = [_masked_softmax(s_win[g], all_keys) for g in gs]
    o_cmp = own([_dot(p_cmp[g], kvc_ref[0, 1, g * nseg:(g + 1) * nseg, :]) for g in gs])
    o_win = own([_dot(p_win[g], vw[g]) for g in gs])
    ps_rows = [jnp.sum(_head_rows_of_group(p_cmp[g], g), axis=0, keepdims=True) for g in gs]
    imp = sum(jnp.dot(part, cover, preferred_element_type=F32) for part in _split3(_pad_rows(ps_rows, 8)))
    score = jnp.where(valid, jnp.where(force, 1e9, imp), -1.0)
    score = jnp.where(lane < n_sel, score, -2.0)
    sel = _select_blocks_ranked(score, G, N_SELECT)
    picked = jnp.dot(_bf(sel), expand, preferred_element_type=F32)
    o_parts = []
    for g in gs:
        kmask = jnp.broadcast_to(picked[g:g + 1], (H, past)) > 0.5
        new_ok = sel[g:g + 1, n_sel - 1:n_sel] > 0.5
        s = jnp.where(kmask, s_sel[g], NEG_BIG)
        sn = jnp.where(new_ok, s_new[g], NEG_BIG)
        m = jnp.maximum(jnp.max(s, axis=1, keepdims=True), sn)
        e = jnp.where(kmask, jnp.exp(s - m), 0.0)
        e_new = jnp.where(new_ok, jnp.exp(sn - m), 0.0)
        den = jnp.maximum(jnp.sum(e, axis=1, keepdims=True) + e_new, 1e-30)
        o_parts.append((_dot(e, vg[g]) + e_new * new_ref[3 * G + g:3 * G + g + 1, :]) / den)
    o_sel = own(o_parts)

    gates = _sigmoid(gate_ref[0])
    hh = lax.broadcasted_iota(jnp.int32, (H, 128), 0)
    ll = lax.broadcasted_iota(jnp.int32, (H, 128), 1)
    gcol = lambda c: jnp.sum(jnp.where(ll == 3 * hh + c, gates, 0.0), axis=1, keepdims=True)
    o_ref[0] = _bf(gcol(0) * o_cmp + gcol(1) * o_sel + gcol(2) * o_win)


def nsa_step(q, gates, cosf, sinf, kvc, kv_new, cache_sel, cache_win, page_table):
    b, n_pages = page_table.shape
    page = cache_sel.shape[1]
    wb = cache_win.shape[1]
    H, G, D = NSA_HEADS, NSA_GROUPS, HEAD_DIM

    rows_pp, rows_w = page * 2 * G, wb * 2 * G
    cache_sel = cache_sel.reshape(cache_sel.shape[0] * rows_pp, D)
    cache_win2 = cache_win.reshape(b * rows_w, D)

    def page_spec(p):
        return pl.BlockSpec((rows_pp, D), lambda i, pt: (pt[i, p], 0))

    per_seq = lambda *shape: pl.BlockSpec((1,) + shape, lambda i, pt: (i,) + (0,) * len(shape))
    flat_seq = lambda rows: pl.BlockSpec((rows, D), lambda i, pt: (i, 0))
    const = lambda *shape: pl.BlockSpec(shape, lambda i, pt: (0,) * len(shape))
    o, nwin = pl.pallas_call(
        functools.partial(_nsa_step_kernel, n_pages=n_pages),
        name="nsa_step",
        grid_spec=pltpu.PrefetchScalarGridSpec(
            num_scalar_prefetch=1,
            grid=(b,),
            in_specs=[page_spec(p) for p in range(n_pages)] + [
                per_seq(H, D), per_seq(1, 128), const(1, D), const(1, D),
                per_seq(2, kvc.shape[2], D), flat_seq(6 * G), flat_seq(rows_w)],
            out_specs=[per_seq(H, D), flat_seq(rows_w)],
        ),
        out_shape=[jax.ShapeDtypeStruct((b, H, D), BF16), jax.ShapeDtypeStruct((b * rows_w, D), F32)],
        compiler_params=_cparams("parallel"),
    )(page_table, *([cache_sel] * n_pages), q.reshape(b, H, D), gates.reshape(b, 1, 128), cosf, sinf,
      kvc, kv_new.reshape(b * 6 * G, D), cache_win2)
    return o.reshape(b, H * D), nwin.reshape(cache_win.shape)


def kernel(x_prompt, x_sample, state_gdn, state_gdn_conv, state_ffn_conv, cache_cmp_kv, cache_sel_kv, cache_win_kv, page_table, norm_mixer, norm_ffn, norm_kv, norm_final, gdn_w_in, gdn_conv_w, gdn_A_log, gdn_dt_bias, gdn_norm_w, gdn_w_out, nsa_w_q, nsa_w_out, kv_w, cmp_pe_k, cmp_pe_v, cmp_w1_k, cmp_w2_k, cmp_w1_v, cmp_w2_v, ffn_w_gate, ffn_w_up, ffn_conv_w, ffn_conv_b, ffn_w_down):
    cmp_pe = jnp.stack([cmp_pe_k, cmp_pe_v])
    cmp_w1 = jnp.stack([cmp_w1_k, cmp_w1_v])
    cmp_w2 = jnp.stack([cmp_w2_k, cmp_w2_v])

    x = x_prompt[0]
    t = x.shape[0]
    x, gdn_s_p, gdn_c_p = gdn_layer_seq(x, norm_mixer[0], gdn_w_in[0], gdn_conv_w[0], gdn_A_log[0], gdn_dt_bias[0],
                                        gdn_norm_w[0], gdn_w_out[0])
    x, ffn_c0_p = ffn_layer_seq(x, norm_ffn[0], (ffn_w_gate, 0), (ffn_w_up, 0), ffn_conv_w[0], ffn_conv_b[0],
                                (ffn_w_down, 0))
    cosf, sinf = _rope_tables(jnp.arange(t, dtype=jnp.int32))
    kv, kvb = kv_rows(x, norm_kv, kv_w, cosf, sinf)
    kvc = compress_seq(kv, cmp_pe, cmp_w1, cmp_w2)
    x = nsa_layer_seq(x, norm_mixer[1], nsa_w_q[0], nsa_w_out[0], kvb, kvc[0], kvc[1], cosf, sinf)
    y_p, ffn_c1_p = ffn_layer_seq(x, norm_ffn[1], (ffn_w_gate, 1), (ffn_w_up, 1), ffn_conv_w[1], ffn_conv_b[1],
                                  (ffn_w_down, 1), final_nw=norm_final)
    g, hd = NSA_GROUPS, HEAD_DIM
    nrow = 2 * g * hd
    cmp_p = kv[:, :nrow].reshape(1, t, 2, g, hd)
    sel_p = kv[:, nrow:2 * nrow].reshape(1, t, 2, g, hd)
    win_p = kv[t - min(WINDOW, t):, 2 * nrow:].reshape(1, min(WINDOW, t), 2, g, hd)
    assert x_sample.shape[1] == 1
    xs = x_sample[:, 0]
    b = xs.shape[0]
    n_pool, page = cache_sel_kv.shape[:2]
    past_len = page_table.shape[1] * page
    wb = cache_win_kv.shape[1]
    xs, gdn_s_s, gdn_c_s = gdn_layer_step(xs, norm_mixer[0], gdn_w_in[0], gdn_conv_w[0], gdn_A_log[0], gdn_dt_bias[0],
                                          gdn_norm_w[0], gdn_w_out[0], state_gdn_conv[0], state_gdn[0])
    xs, ffn_c0_s = ffn_layer_step(xs, norm_ffn[0], (ffn_w_gate, 0), (ffn_w_up, 0), ffn_conv_w[0], ffn_conv_b[0],
                                  (ffn_w_down, 0), state_ffn_conv[0])
    cos1, sin1 = _rope_tables(jnp.full((1,), past_len, jnp.int32))
    kv_s, _ = kv_rows(xs, norm_kv, kv_w, jnp.broadcast_to(cos1, (b, hd)), jnp.broadcast_to(sin1, (b, hd)))
    kvc_s = compress_pages(cache_cmp_kv, page_table, cmp_pe, _bf(cmp_w1), cmp_w2)
    nq = NSA_HEADS * hd
    q_s = rms_matmul(xs, norm_mixer[1], nsa_w_q[0], nq, 512)
    gates_s = rms_matmul(xs, norm_mixer[1], _pad_cols(nsa_w_q[0][:, nq:], 128), 128, 128)
    o_s, win_s = nsa_step(q_s, gates_s, cos1, sin1, kvc_s, kv_s, cache_sel_kv, cache_win_kv, page_table)
    xs = matmul_res(o_s, nsa_w_out[0], xs)
    y_s, ffn_c1_s = ffn_layer_step(xs, norm_ffn[1], (ffn_w_gate, 1), (ffn_w_up, 1), ffn_conv_w[1], ffn_conv_b[1],
                                   (ffn_w_down, 1), state_ffn_conv[1], final_nw=norm_final)
    cmp_s = kv_s[:, :nrow].reshape(b, 1, 2, g, hd)
    sel_s = kv_s[:, nrow:2 * nrow].reshape(b, 1, 2, g, hd)

    return (y_p[None], y_s[:, None],
            gdn_s_p[None, None], gdn_c_p[None, None], jnp.stack([ffn_c0_p, ffn_c1_p])[:, None], cmp_p, sel_p, win_p,
            gdn_s_s[None], gdn_c_s[None], jnp.stack([ffn_c0_s, ffn_c1_s]), cmp_s, sel_s, win_s)
```
